```python
import jax, jax.numpy as jnp
from jax import lax
import numpy as np

D_MODEL = 1024
BATCH = 1
SEQ = 16384
DEPTH = 2
DEC_BATCH = 128
DEC_SEQ = 1
PAST_LEN = 16384
PAGE_SIZE = 128

N_EVEN = (DEPTH + 1) // 2
N_ODD = DEPTH // 2
Q_BLOCK = 128

MLA_HEADS = 8
MLA_D_CQ = 384
MLA_D_C = 256
MLA_D_NOPE = 64
MLA_D_ROPE = 32
MLA_D_V = 64
MLA_SCALE = (MLA_D_NOPE + MLA_D_ROPE) ** -0.5
ROPE_THETA = 10000.0

NSA_HEADS = 8
NSA_KV_HEADS = 2
NSA_GROUP = NSA_HEADS // NSA_KV_HEADS
NSA_HD = 64
NSA_BLOCK = 64
NSA_N_TOP = 15
NSA_WINDOW = 512

DIL_PATTERNS = ((128, 1), (512, 4), (2048, 16))
N_DIL = len(DIL_PATTERNS)
DIL_HEADS = 8
DIL_HD = 64

PEER_HEADS = 8
PEER_N_KEYS = 128
PEER_N_EXPERTS = PEER_N_KEYS * PEER_N_KEYS
PEER_D_KEY = 128
PEER_TOPK = 16
PEER_CHUNK = 128

DN_ALPHA = (2 * DEPTH) ** 0.25
DN_BETA = (8 * DEPTH) ** -0.25
LN_EPS = 1e-5
RMS_EPS = 1e-6
TINY = 1e-30

MLA_COLS = MLA_D_CQ + MLA_D_C + MLA_D_ROPE
NSA_Q_COLS = NSA_HEADS * NSA_HD
NSA_KV_COLS = 3 * 2 * NSA_KV_HEADS * NSA_HD
NSA_GATE_COLS = 3 * NSA_HEADS
EVEN_IN_COLS = MLA_COLS + NSA_Q_COLS + NSA_KV_COLS + NSA_GATE_COLS
EVEN_OUT_COLS = MLA_HEADS * MLA_D_V + NSA_HEADS * NSA_HD
ODD_OUT_COLS = DIL_HEADS * DIL_HD

kernel_name = 'hybrid_mla_nsa_dilated_peer_step'


def layer_norm(x, g, b):
    xf = x.astype(jnp.float32)
    mu = jnp.mean(xf, -1, keepdims=True)
    var = jnp.mean(jnp.square(xf - mu), -1, keepdims=True)
    return ((xf - mu) * lax.rsqrt(var + LN_EPS) * g + b).astype(x.dtype)


def rms_norm(x, g):
    xf = x.astype(jnp.float32)
    return (xf * lax.rsqrt(jnp.mean(xf * xf, -1, keepdims=True) + RMS_EPS) * g).astype(x.dtype)


def alibi_slopes(n):
    return jnp.exp2(-8.0 * jnp.arange(1, n + 1, dtype=jnp.float32) / n)


def rope(x, pos):
    half = x.shape[-1] // 2
    inv = ROPE_THETA ** (-jnp.arange(half, dtype=jnp.float32) / half)
    ang = pos.astype(jnp.float32)[:, None] * inv
    ang = ang.reshape(ang.shape[:1] + (1,) * (x.ndim - 3) + (half,))
    cos, sin = jnp.cos(ang), jnp.sin(ang)
    x1 = x[..., :half].astype(jnp.float32)
    x2 = x[..., half:].astype(jnp.float32)
    return jnp.concatenate([x1 * cos - x2 * sin, x2 * cos + x1 * sin], -1).astype(x.dtype)


def masked_softmax(s, mask):
    s = jnp.where(mask, s, -jnp.inf)
    m = jnp.max(s, axis=-1, keepdims=True)
    m = jnp.where(jnp.isfinite(m), m, 0.0)
    e = jnp.exp(s - m)
    den = jnp.maximum(jnp.sum(e, axis=-1, keepdims=True), TINY)
    return e / den, (m + jnp.log(den))[..., 0]


def pad_to_block(x):
    pad = (-x.shape[1]) % NSA_BLOCK
    return jnp.pad(x, ((0, 0), (0, pad)) + ((0, 0),) * (x.ndim - 2))


def even_project(x, pos, w_in, rms_q, rms_kv, w_uq):
    B, S, _ = x.shape
    h = jnp.einsum('bsd,dc->bsc', x, w_in)
    c_q = rms_norm(h[..., :MLA_D_CQ], rms_q)
    c_kv = rms_norm(h[..., MLA_D_CQ:MLA_D_CQ + MLA_D_C], rms_kv)
    k_rope = rope(h[..., MLA_D_CQ + MLA_D_C:MLA_COLS], pos)
    q = jnp.einsum('bsc,chd->bshd', c_q, w_uq)
    q_nope = q[..., :MLA_D_NOPE]
    q_rope = rope(q[..., MLA_D_NOPE:], pos)
    o = MLA_COLS
    q_nsa = h[..., o:o + NSA_Q_COLS].reshape(B, S, NSA_HEADS, NSA_HD)
    o += NSA_Q_COLS
    kv_nsa = h[..., o:o + NSA_KV_COLS].reshape(B, S, 3, 2, NSA_KV_HEADS, NSA_HD)
    o += NSA_KV_COLS
    gates = jax.nn.sigmoid(h[..., o:o + NSA_GATE_COLS].astype(jnp.float32)).reshape(B, S, NSA_HEADS, 3)
    return q_nope, q_rope, c_kv, k_rope, q_nsa, kv_nsa[:, :, 0], kv_nsa[:, :, 1], kv_nsa[:, :, 2], gates


def mla_prompt(q_nope, q_rope, c_kv, k_rope, w_uk, w_uv):
    B, S = q_nope.shape[:2]
    k_nope = jnp.einsum('bsc,chd->bshd', c_kv, w_uk)
    v = jnp.einsum('bsc,chd->bshd', c_kv, w_uv)
    kpos = jnp.arange(S)

    def block(i):
        q0 = i * Q_BLOCK
        qn = lax.dynamic_slice_in_dim(q_nope, q0, Q_BLOCK, 1)
        qr = lax.dynamic_slice_in_dim(q_rope, q0, Q_BLOCK, 1)
        s = (jnp.einsum('bqhd,bshd->bhqs', qn, k_nope) + jnp.einsum('bqhd,bsd->bhqs', qr, k_rope)).astype(jnp.float32) * MLA_SCALE
        tpos = q0 + jnp.arange(Q_BLOCK)
        p, _ = masked_softmax(s, kpos[None, :] <= tpos[:, None])
        return jnp.einsum('bhqs,bshd->bqhd', p.astype(v.dtype), v)

    o = lax.map(block, jnp.arange(S // Q_BLOCK))
    return jnp.moveaxis(o, 0, 1).reshape(B, S, MLA_HEADS, MLA_D_V)


def mla_sample(q_nope, q_rope, c_new, kr_new, c_past, kr_past, tpos, w_uk, w_uv):
    Q = q_nope.shape[1]
    P = c_past.shape[1]
    q_lat = jnp.einsum('bqhd,chd->bqhc', q_nope, w_uk)
    s_past = jnp.einsum('bqhc,bsc->bhqs', q_lat, c_past) + jnp.einsum('bqhd,bsd->bhqs', q_rope, kr_past)
    s_new = jnp.einsum('bqhc,bsc->bhqs', q_lat, c_new) + jnp.einsum('bqhd,bsd->bhqs', q_rope, kr_new)
    s = jnp.concatenate([s_past, s_new], -1).astype(jnp.float32) * MLA_SCALE
    mask = jnp.concatenate([jnp.ones((Q, P), bool), tpos[None, :] <= tpos[:, None]], -1)
    p, _ = masked_softmax(s, mask)
    p = p.astype(c_past.dtype)
    o_lat = jnp.einsum('bhqs,bsc->bqhc', p[..., :P], c_past) + jnp.einsum('bhqs,bsc->bqhc', p[..., P:], c_new)
    return jnp.einsum('bqhc,chd->bqhd', o_lat, w_uv)


def nsa_compress(kv_rows, pe, w1, w2):
    B, L = kv_rows.shape[:2]
    blocks = kv_rows.reshape(B, L // NSA_BLOCK, NSA_BLOCK, 2, NSA_KV_HEADS, NSA_HD)
    blocks = blocks + jnp.transpose(pe, (1, 0, 2))[None, None, :, :, None, :]
    hid = jax.nn.gelu(jnp.einsum('bnrckd,crde->bncke', blocks, w1), approximate=False)
    return jnp.einsum('bncke,cef->bnckf', hid, w2)


def nsa_core(q, gates, tpos, kc, vc, c_end, fetch_sel, wk, wv, wpos):
    B, Q = q.shape[:2]
    f32 = jnp.float32
    scale = NSA_HD ** -0.5
    qg = q.reshape(B, Q, NSA_KV_HEADS, NSA_GROUP, NSA_HD)
    sl = alibi_slopes(NSA_HEADS).reshape(1, NSA_KV_HEADS, NSA_GROUP, 1, 1)
    dist_c = tpos[:, None] - c_end[None, :]
    ok_c = dist_c >= 0
    s_c = jnp.einsum('bqkgd,bnkd->bkgqn', qg, kc).astype(f32) * scale - sl * dist_c.astype(f32)
    p_c, _ = masked_softmax(s_c, ok_c)
    o_c = jnp.einsum('bkgqn,bnkd->bqkgd', p_c.astype(vc.dtype), vc)
    n_c = kc.shape[1]
    cur = tpos // NSA_BLOCK
    cand = ok_c & (jnp.arange(n_c)[None, :] != cur[:, None])
    imp = jnp.where(cand, jnp.sum(p_c, axis=2), -jnp.inf)
    top_v, top_i = lax.top_k(imp, min(NSA_N_TOP, n_c))
    cur_i = jnp.broadcast_to(cur[None, None, :, None], (B, NSA_KV_HEADS, Q, 1)).astype(top_i.dtype)
    sel_i = jnp.transpose(jnp.concatenate([top_i, cur_i], -1), (0, 2, 1, 3))
    sel_ok = jnp.transpose(jnp.concatenate([jnp.isfinite(top_v), jnp.ones((B, NSA_KV_HEADS, Q, 1), bool)], -1), (0, 2, 1, 3))
    n_sel = sel_i.shape[-1]
    ks, vs = fetch_sel(sel_i)
    ks = ks.reshape(B, Q, NSA_KV_HEADS, n_sel * NSA_BLOCK, NSA_HD)
    vs = vs.reshape(B, Q, NSA_KV_HEADS, n_sel * NSA_BLOCK, NSA_HD)
    spos = (sel_i[..., None] * NSA_BLOCK + jnp.arange(NSA_BLOCK)).reshape(B, Q, NSA_KV_HEADS, n_sel * NSA_BLOCK)
    dist_s = tpos[None, :, None, None] - spos
    ok_s = jnp.repeat(sel_ok, NSA_BLOCK, axis=-1) & (dist_s >= 0)
    dist_s = jnp.transpose(dist_s, (0, 2, 1, 3))[:, :, None]
    ok_s = jnp.transpose(ok_s, (0, 2, 1, 3))[:, :, None]
    s_s = jnp.einsum('bqkgd,bqksd->bkgqs', qg, ks).astype(f32) * scale - sl * dist_s.astype(f32)
    p_s, _ = masked_softmax(s_s, ok_s)
    o_s = jnp.einsum('bkgqs,bqksd->bqkgd', p_s.astype(vs.dtype), vs)
    dist_w = tpos[:, None] - wpos[None, :]
    ok_w = (dist_w >= 0) & (dist_w <= NSA_WINDOW) & (wpos[None, :] >= 0)
    s_w = jnp.einsum('bqkgd,bskd->bkgqs', qg, wk).astype(f32) * scale - sl * dist_w.astype(f32)
    p_w, _ = masked_softmax(s_w, ok_w)
    o_w = jnp.einsum('bkgqs,bskd->bqkgd', p_w.astype(wv.dtype), wv)
    g = gates.reshape(B, Q, NSA_KV_HEADS, NSA_GROUP, 3)
    o = g[..., 0:1] * o_c + g[..., 1:2] * o_s + g[..., 2:3] * o_w
    return o.astype(q.dtype).reshape(B, Q, NSA_HEADS, NSA_HD)


def nsa_prompt(q, gates, kv_c, kv_s, kv_w, pe, w1, w2):
    B, S = q.shape[:2]
    comp = nsa_compress(kv_c, pe, w1, w2)
    n_c = S // NSA_BLOCK
    c_end = (jnp.arange(n_c) + 1) * NSA_BLOCK - 1
    sel_blocks = kv_s.reshape(B, n_c, NSA_BLOCK, 2, NSA_KV_HEADS, NSA_HD)
    b_ix = jnp.arange(B)[:, None, None, None]
    h_ix = jnp.arange(NSA_KV_HEADS)[None, None, :, None]

    def fetch(idx):
        g = sel_blocks[b_ix, idx, :, :, h_ix]
        return g[..., 0, :], g[..., 1, :]

    kv_w_pad = jnp.pad(kv_w, ((0, 0), (NSA_WINDOW, 0), (0, 0), (0, 0), (0, 0)))

    def block(i):
        q0 = i * Q_BLOCK
        qb = lax.dynamic_slice_in_dim(q, q0, Q_BLOCK, 1)
        gb = lax.dynamic_slice_in_dim(gates, q0, Q_BLOCK, 1)
        wb = lax.dynamic_slice_in_dim(kv_w_pad, q0, NSA_WINDOW + Q_BLOCK, 1)
        tpos = q0 + jnp.arange(Q_BLOCK)
        wpos = q0 - NSA_WINDOW + jnp.arange(NSA_WINDOW + Q_BLOCK)
        return nsa_core(qb, gb, tpos, comp[:, :, 0], comp[:, :, 1], c_end, fetch, wb[:, :, 0], wb[:, :, 1], wpos)

    o = lax.map(block, jnp.arange(S // Q_BLOCK))
    return jnp.moveaxis(o, 0, 1).reshape(B, S, NSA_HEADS, NSA_HD)


def nsa_sample(q, gates, tpos, kvc_new, kvs_new, kvw_new, cache_cmp, cache_sel, win_buf, page_table, pe, w1, w2):
    DB = q.shape[0]
    past_len = page_table.shape[1] * PAGE_SIZE
    n_bp = past_len // NSA_BLOCK
    bpp = PAGE_SIZE // NSA_BLOCK
    past_c = cache_cmp[page_table].reshape((DB, past_len) + cache_cmp.shape[2:])
    comp = jnp.concatenate([nsa_compress(past_c, pe, w1, w2), nsa_compress(pad_to_block(kvc_new), pe, w1, w2)], 1)
    n_c = comp.shape[1]
    c_end = (jnp.arange(n_c) + 1) * NSA_BLOCK - 1
    pool_blocks = cache_sel.reshape((-1, NSA_BLOCK) + cache_sel.shape[2:])
    new_blocks = pad_to_block(kvs_new).reshape((DB, -1, NSA_BLOCK) + kvs_new.shape[2:])
    n_bn = new_blocks.shape[1]
    b_ix = jnp.arange(DB)[:, None, None, None]
    h_ix = jnp.arange(NSA_KV_HEADS)[None, None, :, None]

    def fetch(idx):
        jp = jnp.clip(idx, 0, n_bp - 1)
        phys = page_table[b_ix, jp // bpp] * bpp + jp % bpp
        old = pool_blocks[phys, :, :, h_ix]
        jn = jnp.clip(idx - n_bp, 0, n_bn - 1)
        new = new_blocks[b_ix, jn, :, :, h_ix]
        g = jnp.where((idx < n_bp)[..., None, None, None], old, new)
        return g[..., 0, :], g[..., 1, :]

    win = jnp.concatenate([win_buf, kvw_new], 1)
    wpos = past_len - win_buf.shape[1] + jnp.arange(win.shape[1])
    return nsa_core(q, gates, tpos, comp[:, :, 0], comp[:, :, 1], c_end, fetch, win[:, :, 0], win[:, :, 1], wpos)


def even_out(o_mla, o_nsa, w_o):
    B, S = o_mla.shape[:2]
    o = jnp.concatenate([o_mla.reshape(B, S, -1), o_nsa.reshape(B, S, -1)], -1)
    return jnp.einsum('bsc,cd->bsd', o, w_o)


def even_layer_prompt(x, w_in, rms_q, rms_kv, w_uq, w_uk, w_uv, cmp_pe, cmp_w1, cmp_w2, w_o):
    B, S, _ = x.shape
    pos = jnp.arange(S)
    q_nope, q_rope, c_kv, k_rope, q_nsa, kv_c, kv_s, kv_w, gates = even_project(x, pos, w_in, rms_q, rms_kv, w_uq)
    o_mla = mla_prompt(q_nope, q_rope, c_kv, k_rope, w_uk, w_uv)
    o_nsa = nsa_prompt(q_nsa, gates, kv_c, kv_s, kv_w, cmp_pe, cmp_w1, cmp_w2)
    win_rows = kv_w[:, S - min(NSA_WINDOW, S):]
    return even_out(o_mla, o_nsa, w_o), (c_kv, k_rope, kv_c, kv_s, win_rows)


def even_layer_sample(x, c_lat, c_kr, c_cmp, c_sel, win_buf, page_table, w_in, rms_q, rms_kv, w_uq, w_uk, w_uv, cmp_pe, cmp_w1, cmp_w2, w_o):
    DB, Q, _ = x.shape
    pos = PAST_LEN + jnp.arange(Q)
    q_nope, q_rope, c_kv, k_rope, q_nsa, kv_c, kv_s, kv_w, gates = even_project(x, pos, w_in, rms_q, rms_kv, w_uq)
    lat_past = c_lat[page_table].reshape(DB, -1, MLA_D_C)
    kr_past = c_kr[page_table].reshape(DB, -1, MLA_D_ROPE)
    o_mla = mla_sample(q_nope, q_rope, c_kv, k_rope, lat_past, kr_past, pos, w_uk, w_uv)
    o_nsa = nsa_sample(q_nsa, gates, pos, kv_c, kv_s, kv_w, c_cmp, c_sel, win_buf, page_table, cmp_pe, cmp_w1, cmp_w2)
    return even_out(o_mla, o_nsa, w_o), (c_kv, k_rope, kv_c, kv_s, kv_w)


def odd_project(x, w_in):
    h = jnp.einsum('bsd,dgtho->bsgtho', x, w_in)
    return h[:, :, :, 0], h[:, :, :, 1:]


def dilated_attend(q, kv, own_idx, pos_offset, window, dil, slopes):
    n_rows = kv.shape[1]
    taps = jnp.arange(window // dil + 1) * dil
    idx = own_idx[:, None] - taps[None, :]
    ok = (idx >= 0) & (idx + pos_offset >= 0)
    g = kv[:, jnp.clip(idx, 0, n_rows - 1)]
    s = jnp.einsum('bqhd,bqjhd->bhqj', q, g[:, :, :, 0]).astype(jnp.float32) * DIL_HD ** -0.5
    s = s - slopes[None, :, None, None] * taps.astype(jnp.float32)
    p, lse = masked_softmax(s, ok[None, None])
    o = jnp.einsum('bhqj,bqjhd->bqhd', p.astype(kv.dtype), g[:, :, :, 1])
    return o, lse


def merge_dilations(outs):
    o = jnp.stack([r[0] for r in outs])
    w = jax.nn.softmax(jnp.stack([r[1] for r in outs]), axis=0)
    return jnp.einsum('gbhq,gbqhd->bqhd', w.astype(o.dtype), o)


def odd_layer_prompt(x, w_in, w_o):
    B, S, _ = x.shape
    q, kv = odd_project(x, w_in)
    slopes = alibi_slopes(N_DIL * DIL_HEADS).reshape(N_DIL, DIL_HEADS)
    kv_pad = [jnp.pad(kv[:, :, g], ((0, 0), (w, 0), (0, 0), (0, 0), (0, 0))) for g, (w, _) in enumerate(DIL_PATTERNS)]

    def block(i):
        q0 = i * Q_BLOCK
        qb = lax.dynamic_slice_in_dim(q, q0, Q_BLOCK, 1)
        own = jnp.arange(Q_BLOCK)
        outs = []
        for g, (w, d) in enumerate(DIL_PATTERNS):
            kb = lax.dynamic_slice_in_dim(kv_pad[g], q0, w + Q_BLOCK, 1)
            outs.append(dilated_attend(qb[:, :, g], kb, w + own, q0 - w, w, d, slopes[g]))
        return merge_dilations(outs)

    o = lax.map(block, jnp.arange(S // Q_BLOCK))
    o = jnp.moveaxis(o, 0, 1).reshape(B, S, ODD_OUT_COLS)
    y = jnp.einsum('bsc,cd->bsd', o, w_o)
    new = tuple(kv[:, :, g][:, S - min(w, S):] for g, (w, _) in enumerate(DIL_PATTERNS))
    return y, new


def odd_layer_sample(x, bufs, w_in, w_o):
    DB, Q, _ = x.shape
    q, kv = odd_project(x, w_in)
    slopes = alibi_slopes(N_DIL * DIL_HEADS).reshape(N_DIL, DIL_HEADS)
    outs = []
    for g, (w, d) in enumerate(DIL_PATTERNS):
        n_buf = bufs[g].shape[1]
        kloc = jnp.concatenate([bufs[g], kv[:, :, g]], 1)
        outs.append(dilated_attend(q[:, :, g], kloc, n_buf + jnp.arange(Q), PAST_LEN - n_buf, w, d, slopes[g]))
    o = merge_dilations(outs).reshape(DB, Q, ODD_OUT_COLS)
    y = jnp.einsum('bsc,cd->bsd', o, w_o)
    return y, tuple(kv[:, :, g] for g in range(N_DIL))


def peer(x, wq, sub_keys, u, v):
    B, S, D = x.shape
    T = B * S
    n_chunks = -(-T // PEER_CHUNK)
    xt = jnp.pad(x.reshape(T, D), ((0, n_chunks * PEER_CHUNK - T), (0, 0))).reshape(n_chunks, PEER_CHUNK, D)

    def chunk(xc):
        qh = jnp.einsum('td,dhk->thk', xc, wq).reshape(PEER_CHUNK, PEER_HEADS, 2, PEER_D_KEY // 2)
        s = jnp.einsum('thpk,pnk->thpn', qh, sub_keys).astype(jnp.float32)
        v1, i1 = lax.top_k(s[:, :, 0], PEER_TOPK)
        v2, i2 = lax.top_k(s[:, :, 1], PEER_TOPK)
        cand = (v1[..., :, None] + v2[..., None, :]).reshape(PEER_CHUNK, PEER_HEADS, PEER_TOPK * PEER_TOPK)
        best, ci = lax.top_k(cand, PEER_TOPK)
        e = jnp.take_along_axis(i1, ci // PEER_TOPK, -1) * PEER_N_KEYS + jnp.take_along_axis(i2, ci % PEER_TOPK, -1)
        gate = jax.nn.softmax(best, axis=-1)
        ue = u[e]
        ve = v[e]
        act = jax.nn.gelu(jnp.einsum('td,thkd->thk', xc, ue).astype(jnp.float32), approximate=False)
        return jnp.einsum('thk,thkd->td', (gate * act).astype(ve.dtype), ve)

    out = lax.map(chunk, xt).reshape(-1, D)[:T]
    return out.reshape(B, S, D)


def setup_inputs(seed: int = 0) -> dict:
    key = jax.random.key(seed)
    keys = iter(jax.random.split(key, 48))

    def nrm(shape, scale=1.0):
        return jax.random.normal(next(keys), shape, jnp.float32) * scale

    n_pages = PAST_LEN // PAGE_SIZE
    n_used = DEC_BATCH * n_pages
    n_pool = n_used + n_used // 4
    page_table = jax.random.permutation(next(keys), n_pool)[:n_used].reshape(DEC_BATCH, n_pages).astype(jnp.int32)
    nkv = (2, NSA_KV_HEADS, NSA_HD)
    dkv = (2, DIL_HEADS, DIL_HD)
    return {
        'x_prompt': nrm((BATCH, SEQ, D_MODEL)),
        'x_sample': nrm((DEC_BATCH, DEC_SEQ, D_MODEL)),
        'cache_mla_latent': nrm((N_EVEN, n_pool, PAGE_SIZE, MLA_D_C)),
        'cache_mla_krope': nrm((N_EVEN, n_pool, PAGE_SIZE, MLA_D_ROPE)),
        'cache_nsa_cmp': nrm((N_EVEN, n_pool, PAGE_SIZE) + nkv),
        'cache_nsa_sel': nrm((N_EVEN, n_pool, PAGE_SIZE) + nkv),
        'state_nsa_win': nrm((N_EVEN, DEC_BATCH, min(NSA_WINDOW, PAST_LEN)) + nkv),
        'state_dil_w128': nrm((N_ODD, DEC_BATCH, min(DIL_PATTERNS[0][0], PAST_LEN)) + dkv),
        'state_dil_w512': nrm((N_ODD, DEC_BATCH, min(DIL_PATTERNS[1][0], PAST_LEN)) + dkv),
        'state_dil_w2048': nrm((N_ODD, DEC_BATCH, min(DIL_PATTERNS[2][0], PAST_LEN)) + dkv),
        'page_table': page_table,
        'ln_g': 1.0 + nrm((DEPTH, 2, D_MODEL), 0.02),
        'ln_b': nrm((DEPTH, 2, D_MODEL), 0.02),
        'even_w_in': nrm((N_EVEN, D_MODEL, EVEN_IN_COLS), D_MODEL ** -0.5),
        'mla_rms_q': 1.0 + nrm((N_EVEN, MLA_D_CQ), 0.02),
        'mla_rms_kv': 1.0 + nrm((N_EVEN, MLA_D_C), 0.02),
        'mla_w_uq': nrm((N_EVEN, MLA_D_CQ, MLA_HEADS, MLA_D_NOPE + MLA_D_ROPE), MLA_D_CQ ** -0.5),
        'mla_w_uk': nrm((N_EVEN, MLA_D_C, MLA_HEADS, MLA_D_NOPE), MLA_D_C ** -0.5),
        'mla_w_uv': nrm((N_EVEN, MLA_D_C, MLA_HEADS, MLA_D_V), MLA_D_C ** -0.5),
        'nsa_cmp_pe': nrm((N_EVEN, 2, NSA_BLOCK, NSA_HD), 0.02),
        'nsa_cmp_w1': nrm((N_EVEN, 2, NSA_BLOCK, NSA_HD, NSA_HD), (NSA_BLOCK * NSA_HD) ** -0.5),
        'nsa_cmp_w2': nrm((N_EVEN, 2, NSA_HD, NSA_HD), NSA_HD ** -0.5),
        'even_w_o': nrm((N_EVEN, EVEN_OUT_COLS, D_MODEL), DN_BETA * EVEN_OUT_COLS ** -0.5),
        'odd_w_in': nrm((N_ODD, D_MODEL, N_DIL, 3, DIL_HEADS, DIL_HD), D_MODEL ** -0.5),
        'odd_w_o': nrm((N_ODD, ODD_OUT_COLS, D_MODEL), DN_BETA * ODD_OUT_COLS ** -0.5),
        'peer_wq': nrm((DEPTH, D_MODEL, PEER_HEADS, PEER_D_KEY), D_MODEL ** -0.5),
        'peer_sub_keys': nrm((DEPTH, 2, PEER_N_KEYS, PEER_D_KEY // 2), (PEER_D_KEY // 2) ** -0.5),
        'peer_u': nrm((DEPTH, PEER_N_EXPERTS, D_MODEL), D_MODEL ** -0.5),
        'peer_v': nrm((DEPTH, PEER_N_EXPERTS, D_MODEL), DN_BETA * PEER_HEADS ** -0.5),
    }


def reference(x_prompt, x_sample, cache_mla_latent, cache_mla_krope, cache_nsa_cmp, cache_nsa_sel, state_nsa_win, state_dil_w128, state_dil_w512, state_dil_w2048, page_table, ln_g, ln_b, even_w_in, mla_rms_q, mla_rms_kv, mla_w_uq, mla_w_uk, mla_w_uv, nsa_cmp_pe, nsa_cmp_w1, nsa_cmp_w2, even_w_o, odd_w_in, odd_w_o, peer_wq, peer_sub_keys, peer_u, peer_v):
    xp, xs = x_prompt, x_sample
    dil_states = (state_dil_w128, state_dil_w512, state_dil_w2048)
    ev_p = [[] for _ in range(5)]
    ev_s = [[] for _ in range(5)]
    od_p = [[] for _ in range(N_DIL)]
    od_s = [[] for _ in range(N_DIL)]
    for layer in range(DEPTH):
        i = layer // 2
        if layer % 2 == 0:
            ew = (even_w_in[i], mla_rms_q[i], mla_rms_kv[i], mla_w_uq[i], mla_w_uk[i], mla_w_uv[i],
                  nsa_cmp_pe[i], nsa_cmp_w1[i], nsa_cmp_w2[i], even_w_o[i])
            mp, new_p = even_layer_prompt(xp, *ew)
            ms, new_s = even_layer_sample(xs, cache_mla_latent[i], cache_mla_krope[i], cache_nsa_cmp[i], cache_nsa_sel[i],
                                          state_nsa_win[i], page_table, *ew)
            for j in range(5):
                ev_p[j].append(new_p[j])
                ev_s[j].append(new_s[j])
        else:
            mp, new_p = odd_layer_prompt(xp, odd_w_in[i], odd_w_o[i])
            ms, new_s = odd_layer_sample(xs, tuple(st[i] for st in dil_states), odd_w_in[i], odd_w_o[i])
            for j in range(N_DIL):
                od_p[j].append(new_p[j])
                od_s[j].append(new_s[j])
        xp = layer_norm(DN_ALPHA * xp + mp, ln_g[layer, 0], ln_b[layer, 0])
        xs = layer_norm(DN_ALPHA * xs + ms, ln_g[layer, 0], ln_b[layer, 0])
        xp = layer_norm(DN_ALPHA * xp + peer(xp, peer_wq[layer], peer_sub_keys[layer], peer_u[layer], peer_v[layer]), ln_g[layer, 1], ln_b[layer, 1])
        xs = layer_norm(DN_ALPHA * xs + peer(xs, peer_wq[layer], peer_sub_keys[layer], peer_u[layer], peer_v[layer]), ln_g[layer, 1], ln_b[layer, 1])
    return (xp, xs,
            jnp.stack(ev_p[0]), jnp.stack(ev_s[0]),
            jnp.stack(ev_p[1]), jnp.stack(ev_s[1]),
            jnp.stack(ev_p[2]), jnp.stack(ev_s[2]),
            jnp.stack(ev_p[3]), jnp.stack(ev_s[3]),
            jnp.stack(ev_p[4]), jnp.stack(ev_s[4]),
            jnp.stack(od_p[0]), jnp.stack(od_s[0]),
            jnp.stack(od_p[1]), jnp.stack(od_s[1]),
            jnp.stack(od_p[2]), jnp.stack(od_s[2]))
```

```python
import functools
import math

import numpy as np
import jax
import jax.numpy as jnp
from jax import lax
from jax.experimental import pallas as pl
from jax.experimental.pallas import tpu as pltpu

D_MODEL = 1024
DEPTH = 2
PAGE_SIZE = 128

MLA_HEADS = 8
MLA_D_CQ = 384
MLA_D_C = 256
MLA_D_NOPE = 64
MLA_D_ROPE = 32
MLA_D_V = 64
MLA_D_QK = MLA_D_NOPE + MLA_D_ROPE
MLA_SCALE = MLA_D_QK ** -0.5
ROPE_THETA = 10000.0

NSA_HEADS = 8
NSA_KV_HEADS = 2
NSA_GROUP = NSA_HEADS // NSA_KV_HEADS
NSA_HD = 64
NSA_BLOCK = 64
NSA_N_TOP = 15
NSA_WINDOW = 512

DIL_PATTERNS = ((128, 1), (512, 4), (2048, 16))
N_DIL = len(DIL_PATTERNS)
DIL_HEADS = 8
DIL_HD = 64

PEER_HEADS = 8
PEER_N_KEYS = 128
PEER_D_KEY = 128
PEER_TOPK = 16

DN_ALPHA = (2 * DEPTH) ** 0.25
LN_EPS = 1e-5
RMS_EPS = 1e-6
TINY = 1e-30
NEG_BIG = -1e30

MLA_COLS = MLA_D_CQ + MLA_D_C + MLA_D_ROPE
NSA_Q_COLS = NSA_HEADS * NSA_HD
NSA_KV_COLS = 3 * 2 * NSA_KV_HEADS * NSA_HD
NSA_GATE_COLS = 3 * NSA_HEADS
ODD_OUT_COLS = DIL_HEADS * DIL_HD

LANE = 128
VMEM_LIMIT = 48 * 1024 * 1024

BF16 = jnp.bfloat16
F32 = jnp.float32


def _round_up(x, m):
    return -(-x // m) * m


def _tile(n, pref):
    if n <= pref:
        return n
    t = pref
    while n % t:
        t //= 2
    return t


def _params(sem):
    return pltpu.CompilerParams(dimension_semantics=sem, vmem_limit_bytes=VMEM_LIMIT)


def _resid_ln_body(x_ref, m_ref, g_ref, b_ref, o_ref):
    z = DN_ALPHA * x_ref[...] + m_ref[...]
    mu = jnp.mean(z, -1, keepdims=True)
    zc = z - mu
    var = jnp.mean(zc * zc, -1, keepdims=True)
    o_ref[...] = zc * lax.rsqrt(var + LN_EPS) * g_ref[...] + b_ref[...]


def resid_layer_norm(x, m, g, b):
    T, D = x.shape
    tt = _tile(T, 512)
    return pl.pallas_call(
        _resid_ln_body,
        grid=(T // tt,),
        in_specs=[pl.BlockSpec((tt, D), lambda i: (i, 0)),
                  pl.BlockSpec((tt, D), lambda i: (i, 0)),
                  pl.BlockSpec((1, D), lambda i: (0, 0)),
                  pl.BlockSpec((1, D), lambda i: (0, 0))],
        out_specs=pl.BlockSpec((tt, D), lambda i: (i, 0)),
        out_shape=jax.ShapeDtypeStruct((T, D), F32),
        compiler_params=_params(("parallel",)),
        name="resid_ln",
    )(x, m, g.reshape(1, D), b.reshape(1, D))


def _mm_body(a_ref, b_ref, o_ref):
    o_ref[...] = jnp.dot(a_ref[...].astype(BF16), b_ref[...], preferred_element_type=F32).astype(o_ref.dtype)


def matmul(a, b, *, tm=512, tn=None, out_dtype=F32):
    M, K = a.shape
    N = b.shape[1]
    tm = _tile(M, tm)
    tn = N if tn is None else _tile(N, tn)
    return pl.pallas_call(
        _mm_body,
        grid=(M // tm, N // tn),
        in_specs=[pl.BlockSpec((tm, K), lambda i, j: (i, 0)),
                  pl.BlockSpec((K, tn), lambda i, j: (0, j))],
        out_specs=pl.BlockSpec((tm, tn), lambda i, j: (i, j)),
        out_shape=jax.ShapeDtypeStruct((M, N), out_dtype),
        compiler_params=_params(("parallel", "parallel")),
        name="matmul",
    )(a, b.astype(BF16))


def _flash_body(qi_ref, kj_ref, fl_ref, *refs, H, KVH, Dk, Dv, TQ, TK, scale, slopes, mode, band,
                use_sel, nbp, want_lse):
    refs = list(refs)
    q_ref, k_ref, v_ref = refs[:3]
    refs = refs[3:]
    if use_sel:
        sel_ref, exp_ref = refs[:2]
        refs = refs[2:]
    o_ref = refs[0]
    refs = refs[1:]
    if want_lse:
        lse_ref = refs[0]
        refs = refs[1:]
    m_sc, l_sc, acc_sc = refs
    G = H // KVH
    step = pl.program_id(1)
    qi = qi_ref[step]
    kj = kj_ref[step]
    fl = fl_ref[step]

    @pl.when((fl & 1) != 0)
    def _init():
        m_sc[...] = jnp.full(m_sc.shape, NEG_BIG, F32)
        l_sc[...] = jnp.zeros(l_sc.shape, F32)
        acc_sc[...] = jnp.zeros(acc_sc.shape, F32)

    qpos = qi * TQ + lax.broadcasted_iota(jnp.int32, (TQ, TK), 0)
    kpos = kj * TK + lax.broadcasted_iota(jnp.int32, (TQ, TK), 1)
    rel_i = qpos - kpos
    valid = rel_i >= 0
    if mode == "band":
        valid = valid & (rel_i <= band)
    rel = rel_i.astype(F32)
    for kv in range(KVH):
        k = k_ref[:, kv * Dk:(kv + 1) * Dk]
        v = v_ref[:, kv * Dv:(kv + 1) * Dv]
        ok = valid
        if use_sel:
            hit = jnp.dot(sel_ref[:, kv * nbp:(kv + 1) * nbp], exp_ref[...], preferred_element_type=F32)
            ok = ok & (hit > 0.5)
        bias0 = jnp.where(ok, 0.0, NEG_BIG)
        for g in range(G):
            h = kv * G + g
            q = q_ref[:, h * Dk:(h + 1) * Dk]
            s = lax.dot_general(q, k, (((1,), (1,)), ((), ())), preferred_element_type=F32)
            if slopes[h] != 0.0:
                s = s * scale + (bias0 - slopes[h] * rel)
            else:
                s = s * scale + bias0
            m_old = m_sc[h]
            m_new = jnp.maximum(m_old, jnp.max(s, axis=-1, keepdims=True))
            alpha = jnp.exp(m_old - m_new)
            p = jnp.exp(s - m_new)
            l_sc[h] = alpha * l_sc[h] + jnp.sum(p, axis=-1, keepdims=True)
            acc_sc[h] = alpha * acc_sc[h] + jnp.dot(p.astype(BF16), v, preferred_element_type=F32)
            m_sc[h] = m_new

    @pl.when((fl & 2) != 0)
    def _fin():
        for h in range(H):
            l = l_sc[h]
            o_ref[:, h * Dv:(h + 1) * Dv] = acc_sc[h] / l
            if want_lse:
                lse_ref[:, h:h + 1] = m_sc[h] + jnp.log(l)


def flash_attention(q, k, v, *, H, KVH, Dk, Dv, R=1, q_col=None, k_col=None, v_col=None, scale, slopes=None,
                    mode="causal", band=0, sel=None, tq=256, tk=512, want_lse=False):
    Sq = q.shape[0]
    Sk = k.shape[0]
    TQ = _tile(Sq, tq)
    TK = _tile(Sk, tk)
    nq = Sq // TQ
    slopes = tuple(0.0 for _ in range(H)) if slopes is None else tuple(float(s) for s in slopes)
    q_col = q_col or (lambda r: r)
    k_col = k_col or (lambda r: r)
    v_col = v_col or (lambda r: r)
    qi_l, kj_l, fl_l = [], [], []
    for i in range(nq):
        hi = ((i + 1) * TQ - 1) // TK
        lo = 0 if mode == "causal" else max(0, (i * TQ - band) // TK)
        for j in range(lo, hi + 1):
            qi_l.append(i)
            kj_l.append(j)
            fl_l.append((1 if j == lo else 0) | (2 if j == hi else 0))
    n_steps = len(qi_l)
    qi_a = jnp.asarray(np.array(qi_l, np.int32))
    kj_a = jnp.asarray(np.array(kj_l, np.int32))
    fl_a = jnp.asarray(np.array(fl_l, np.int32))

    in_specs = [pl.BlockSpec((TQ, H * Dk), lambda r, s, qi, kj, fl: (qi[s], q_col(r))),
                pl.BlockSpec((TK, KVH * Dk), lambda r, s, qi, kj, fl: (kj[s], k_col(r))),
                pl.BlockSpec((TK, KVH * Dv), lambda r, s, qi, kj, fl: (kj[s], v_col(r)))]
    args = [q, k, v]
    use_sel = sel is not None
    nbp = 0
    if use_sel:
        nbp = sel.shape[1] // KVH
        nk = Sk // TK
        blk = (np.arange(nk)[:, None, None] * (TK // NSA_BLOCK) + np.arange(TK)[None, None, :] // NSA_BLOCK)
        expand = (np.arange(nbp)[None, :, None] == blk).astype(np.float32)
        in_specs += [pl.BlockSpec((TQ, KVH * nbp), lambda r, s, qi, kj, fl: (qi[s], 0)),
                     pl.BlockSpec((None, nbp, TK), lambda r, s, qi, kj, fl: (kj[s], 0, 0))]
        args += [sel, jnp.asarray(expand, BF16)]
    out_shape = [jax.ShapeDtypeStruct((Sq, R * H * Dv), F32)]
    out_specs = [pl.BlockSpec((TQ, H * Dv), lambda r, s, qi, kj, fl: (qi[s], r))]
    if want_lse:
        out_shape.append(jax.ShapeDtypeStruct((R, Sq, H), F32))
        out_specs.append(pl.BlockSpec((None, TQ, H), lambda r, s, qi, kj, fl: (r, qi[s], 0)))
    body = functools.partial(_flash_body, H=H, KVH=KVH, Dk=Dk, Dv=Dv, TQ=TQ, TK=TK, scale=float(scale),
                             slopes=slopes, mode=mode, band=int(band), use_sel=use_sel, nbp=nbp,
                             want_lse=want_lse)
    res = pl.pallas_call(
        body,
        grid_spec=pltpu.PrefetchScalarGridSpec(
            num_scalar_prefetch=3, grid=(R, n_steps), in_specs=in_specs, out_specs=out_specs,
            scratch_shapes=[pltpu.VMEM((H, TQ, 1), F32), pltpu.VMEM((H, TQ, 1), F32),
                            pltpu.VMEM((H, TQ, Dv), F32)]),
        out_shape=out_shape,
        compiler_params=_params(("parallel", "arbitrary")),
        name="flash_" + mode + ("_sel" if use_sel else ""),
    )(qi_a, kj_a, fl_a, *args)
    return res if want_lse else res[0]


def _nsa_cmp_body(q_ref, kc_ref, vc_ref, o_ref, sel_ref, *, TQ, NBP, n_blocks, slopes):
    i = pl.program_id(0)
    tpos = i * TQ + lax.broadcasted_iota(jnp.int32, (TQ, NBP), 0)
    blk = lax.broadcasted_iota(jnp.int32, (TQ, NBP), 1)
    dist_i = tpos - ((blk + 1) * NSA_BLOCK - 1)
    ok = (dist_i >= 0) & (blk < n_blocks)
    dist = dist_i.astype(F32)
    cur = jnp.right_shift(tpos, NSA_BLOCK.bit_length() - 1)
    cand = ok & (blk != cur)
    scale = NSA_HD ** -0.5
    for kv in range(NSA_KV_HEADS):
        kc = kc_ref[:, kv * NSA_HD:(kv + 1) * NSA_HD]
        vc = vc_ref[:, kv * NSA_HD:(kv + 1) * NSA_HD]
        imp = jnp.zeros((TQ, NBP), F32)
        for g in range(NSA_GROUP):
            h = kv * NSA_GROUP + g
            q = q_ref[:, h * NSA_HD:(h + 1) * NSA_HD]
            s = lax.dot_general(q, kc, (((1,), (1,)), ((), ())), preferred_element_type=F32)
            s = s * scale - slopes[h] * dist
            s = jnp.where(ok, s, NEG_BIG)
            m = jnp.max(s, axis=-1, keepdims=True)
            e = jnp.where(ok, jnp.exp(s - m), 0.0)
            den = jnp.maximum(jnp.sum(e, axis=-1, keepdims=True), TINY)
            p = e / den
            o_ref[:, h * NSA_HD:(h + 1) * NSA_HD] = jnp.dot(p.astype(BF16), vc, preferred_element_type=F32)
            imp = imp + p
        work = jnp.where(cand, imp, -jnp.inf)
        picked = jnp.zeros((TQ, NBP), F32)
        for _ in range(NSA_N_TOP):
            mx = jnp.max(work, axis=-1, keepdims=True)
            first = jnp.min(jnp.where(work == mx, blk, NBP), axis=-1, keepdims=True)
            hit = blk == first
            picked = jnp.where(hit & (mx > -jnp.inf), 1.0, picked)
            work = jnp.where(hit, -jnp.inf, work)
        picked = jnp.where(blk == cur, 1.0, picked)
        sel_ref[:, kv * NBP:(kv + 1) * NBP] = picked.astype(BF16)


def nsa_cmp_select(q, kc, vc, n_blocks, *, tq=256):
    S = q.shape[0]
    NBP = kc.shape[0]
    TQ = _tile(S, tq)
    slopes = tuple(2.0 ** (-8.0 * (h + 1) / NSA_HEADS) for h in range(NSA_HEADS))
    body = functools.partial(_nsa_cmp_body, TQ=TQ, NBP=NBP, n_blocks=n_blocks, slopes=slopes)
    return pl.pallas_call(
        body,
        grid=(S // TQ,),
        in_specs=[pl.BlockSpec((TQ, NSA_Q_COLS), lambda i: (i, 0)),
                  pl.BlockSpec((NBP, NSA_KV_HEADS * NSA_HD), lambda i: (0, 0)),
                  pl.BlockSpec((NBP, NSA_KV_HEADS * NSA_HD), lambda i: (0, 0))],
        out_specs=[pl.BlockSpec((TQ, NSA_Q_COLS), lambda i: (i, 0)),
                   pl.BlockSpec((TQ, NSA_KV_HEADS * NBP), lambda i: (i, 0))],
        out_shape=[jax.ShapeDtypeStruct((S, NSA_Q_COLS), F32),
                   jax.ShapeDtypeStruct((S, NSA_KV_HEADS * NBP), BF16)],
        compiler_params=_params(("parallel",)),
        name="nsa_cmp_select",
    )(q, kc, vc)


def _top_rows(work, n_rows, k):
    iota = lax.broadcasted_iota(jnp.int32, work.shape, 0)
    vals, idxs = [], []
    for _ in range(k):
        mx = jnp.max(work, axis=0, keepdims=True)
        first = jnp.min(jnp.where(work == mx, iota, n_rows), axis=0, keepdims=True)
        vals.append(mx)
        idxs.append(first)
        work = jnp.where(iota == first, -jnp.inf, work)
    return jnp.concatenate(vals, axis=0), jnp.concatenate(idxs, axis=0)


def _peer_route_body(qh_ref, sk_ref, a_ref, b_ref, g_ref, *, TT):
    K = PEER_TOPK
    half = PEER_D_KEY // 2
    for h in range(PEER_HEADS):
        tops = []
        for p in range(2):
            c0 = (h * 2 + p) * half
            qhp = qh_ref[:, c0:c0 + half].astype(BF16)
            s = lax.dot_general(sk_ref[p], qhp, (((1,), (1,)), ((), ())), preferred_element_type=F32)
            tops.append(_top_rows(s, PEER_N_KEYS, K))
        (v1, i1), (v2, i2) = tops
        cand = jnp.concatenate([v1[k1:k1 + 1, :] + v2 for k1 in range(K)], axis=0)
        best, ci = _top_rows(cand, K * K, K)
        hi = jnp.right_shift(ci, K.bit_length() - 1)
        lo = jnp.bitwise_and(ci, K - 1)
        a = jnp.zeros((K, TT), jnp.int32)
        b = jnp.zeros((K, TT), jnp.int32)
        for k in range(K):
            a = jnp.where(hi == k, i1[k:k + 1, :], a)
            b = jnp.where(lo == k, i2[k:k + 1, :], b)
        e = jnp.exp(best - jnp.max(best, axis=0, keepdims=True))
        gate = e / jnp.sum(e, axis=0, keepdims=True)
        a_ref[h * K:(h + 1) * K, :] = a.astype(F32)
        b_ref[h * K:(h + 1) * K, :] = b.astype(F32)
        g_ref[h * K:(h + 1) * K, :] = gate


def peer_route(qh, sub_keys):
    T = qh.shape[0]
    TT = _tile(T, 128)
    HK = PEER_HEADS * PEER_TOPK
    out = jax.ShapeDtypeStruct((HK, T), F32)
    return pl.pallas_call(
        functools.partial(_peer_route_body, TT=TT),
        grid=(T // TT,),
        in_specs=[pl.BlockSpec((TT, PEER_HEADS * PEER_D_KEY), lambda i: (i, 0)),
                  pl.BlockSpec((2, PEER_N_KEYS, PEER_D_KEY // 2), lambda i: (0, 0, 0))],
        out_specs=[pl.BlockSpec((HK, TT), lambda i: (0, i))] * 3,
        out_shape=[out, out, out],
        compiler_params=_params(("parallel",)),
        name="peer_route",
    )(qh, sub_keys.astype(BF16))


def _peer_gate_body(a_ref, b_ref, g_ref, o_ref, at_sc, bt_sc, gt_sc, gm_sc, *, TT):
    NK = PEER_N_KEYS
    at_sc[...] = a_ref[...].T
    bt_sc[...] = b_ref[...].T
    gt_sc[...] = g_ref[...].T
    CH = 8

    def chunk(c, carry):
        r0 = pl.multiple_of(c * CH, CH)
        a = at_sc[pl.ds(r0, CH), :]
        b = bt_sc[pl.ds(r0, CH), :]
        g = gt_sc[pl.ds(r0, CH), :]
        key = lax.broadcasted_iota(jnp.int32, (CH, NK, a.shape[1]), 1).astype(F32)
        ga = jnp.where(a[:, None, :] == key, g[:, None, :], 0.0).astype(BF16)
        ob = jnp.where(b[:, None, :] == key, 1.0, 0.0).astype(BF16)
        gm = jnp.einsum('tik,tjk->tij', ga, ob, preferred_element_type=F32)
        gm_sc[pl.ds(pl.multiple_of(c * CH * NK, CH * NK), CH * NK), :] = gm.reshape(CH * NK, NK)
        return carry

    lax.fori_loop(0, TT // CH, chunk, 0)
    for i1 in range(NK):
        o_ref[:, i1 * NK:(i1 + 1) * NK] = gm_sc[pl.ds(i1, TT, stride=NK), :].astype(o_ref.dtype)


def peer_gate_matrix(a, b, g):
    HK, T = a.shape
    TT = _tile(T, 128)
    NE = PEER_N_KEYS * PEER_N_KEYS
    return pl.pallas_call(
        functools.partial(_peer_gate_body, TT=TT),
        grid=(T // TT,),
        in_specs=[pl.BlockSpec((HK, TT), lambda i: (0, i))] * 3,
        out_specs=pl.BlockSpec((TT, NE), lambda i: (i, 0)),
        out_shape=jax.ShapeDtypeStruct((T, NE), BF16),
        scratch_shapes=[pltpu.VMEM((TT, HK), F32)] * 3 + [pltpu.VMEM((TT * PEER_N_KEYS, PEER_N_KEYS), F32)],
        compiler_params=_params(("parallel",)),
        name="peer_gate_matrix",
    )(a, b, g)


def _peer_dense_body(x_ref, ut_ref, v_ref, g_ref, o_ref, acc_sc):
    j = pl.program_id(1)

    @pl.when(j == 0)
    def _init():
        acc_sc[...] = jnp.zeros(acc_sc.shape, F32)

    h = jnp.dot(x_ref[...], ut_ref[...], preferred_element_type=F32)
    act = 0.5 * h * (1.0 + lax.erf(h * (2.0 ** -0.5)))
    w = (act * g_ref[...].astype(F32)).astype(BF16)
    acc_sc[...] += jnp.dot(w, v_ref[...], preferred_element_type=F32)

    @pl.when(j == pl.num_programs(1) - 1)
    def _fin():
        o_ref[...] = acc_sc[...]


def peer_dense(x, u_t, v, gmat, *, tt=512, te=512):
    T, D = x.shape
    NE = v.shape[0]
    TT = _tile(T, tt)
    TE = _tile(NE, te)
    return pl.pallas_call(
        _peer_dense_body,
        grid=(T // TT, NE // TE),
        in_specs=[pl.BlockSpec((TT, D), lambda i, j: (i, 0)),
                  pl.BlockSpec((D, TE), lambda i, j: (0, j)),
                  pl.BlockSpec((TE, D), lambda i, j: (j, 0)),
                  pl.BlockSpec((TT, TE), lambda i, j: (i, j))],
        out_specs=pl.BlockSpec((TT, D), lambda i, j: (i, 0)),
        out_shape=jax.ShapeDtypeStruct((T, D), F32),
        scratch_shapes=[pltpu.VMEM((TT, D), F32)],
        compiler_params=_params(("parallel", "arbitrary")),
        name="peer_dense",
    )(x, u_t, v, gmat)


def peer(x, wq, sub_keys, u_t, v_b):
    T, D = x.shape
    qh = matmul(x, wq.reshape(D, PEER_HEADS * PEER_D_KEY))
    a, b, g = peer_route(qh, sub_keys)
    gmat = peer_gate_matrix(a, b, g)
    return peer_dense(x.astype(BF16), u_t, v_b, gmat)


def _slab_attend_body(qT_ref, newT_ref, st_ref, o_ref, lse_ref, *, H, W, hd, k_off, v_off, nk_col, nv_col,
                      stride, slopes, scale):
    lane = lax.broadcasted_iota(jnp.int32, (1, W), 1)
    dist = (W - lane).astype(F32)
    on_tap = (lane & (stride - 1)) == 0
    for h in range(H):
        qh = qT_ref[:, h:h + 1]
        kt = st_ref[k_off[h]:k_off[h] + hd, :]
        vt = st_ref[v_off[h]:v_off[h] + hd, :]
        s = jnp.sum(kt * qh, axis=0, keepdims=True) * scale - slopes[h] * dist
        s = jnp.where(on_tap, s, NEG_BIG)
        s0 = jnp.sum(newT_ref[:, nk_col[h]:nk_col[h] + 1] * qh, axis=0, keepdims=True) * scale
        m = jnp.maximum(jnp.max(s, axis=-1, keepdims=True), s0)
        p = jnp.where(on_tap, jnp.exp(s - m), 0.0)
        p0 = jnp.exp(s0 - m)
        l = jnp.sum(p, axis=-1, keepdims=True) + p0
        o = jnp.sum(vt * p, axis=-1, keepdims=True) + p0 * newT_ref[:, nv_col[h]:nv_col[h] + 1]
        o_ref[:, h:h + 1] = o / l
        lse_ref[:, h:h + 1] = m + jnp.log(l)


def slab_attend(qT, newT, slab, *, H, hd, k_off, v_off, nk_col, nv_col, stride, slopes, scale):
    DB, F, W = slab.shape
    body = functools.partial(_slab_attend_body, H=H, W=W, hd=hd, k_off=tuple(k_off), v_off=tuple(v_off),
                             nk_col=tuple(nk_col), nv_col=tuple(nv_col), stride=stride,
                             slopes=tuple(float(s) for s in slopes), scale=float(scale))
    return pl.pallas_call(
        body,
        grid=(DB,),
        in_specs=[pl.BlockSpec((None, hd, H), lambda b: (b, 0, 0)),
                  pl.BlockSpec((None, hd, newT.shape[2]), lambda b: (b, 0, 0)),
                  pl.BlockSpec((None, F, W), lambda b: (b, 0, 0))],
        out_specs=[pl.BlockSpec((None, hd, H), lambda b: (b, 0, 0)),
                   pl.BlockSpec((None, 1, H), lambda b: (b, 0, 0))],
        out_shape=[jax.ShapeDtypeStruct((DB, hd, H), F32), jax.ShapeDtypeStruct((DB, 1, H), F32)],
        compiler_params=_params(("parallel",)),
        name="slab_attend",
    )(qT, newT, slab)


def _nsa_sel_sample_body(pid_ref, half_ref, ok_ref, blk_ref, qT_ref, newT_ref, p0_ref, p1_ref, o_ref,
                         m_sc, l_sc, acc_sc, *, n_sel, tpos, slopes, scale):
    b = pl.program_id(0)
    n = pl.program_id(1)
    hd = NSA_HD

    @pl.when(n == 0)
    def _init():
        m_sc[...] = jnp.full(m_sc.shape, NEG_BIG, F32)
        l_sc[...] = jnp.zeros(l_sc.shape, F32)
        acc_sc[...] = jnp.zeros(acc_sc.shape, F32)

    lane = lax.broadcasted_iota(jnp.int32, (1, PAGE_SIZE), 1)
    for k, page in enumerate((p0_ref, p1_ref)):
        idx = (b * NSA_KV_HEADS + k) * n_sel + n
        mask = (jnp.right_shift(lane, NSA_BLOCK.bit_length() - 1) == half_ref[idx]) & (ok_ref[idx] != 0)
        kpos = blk_ref[idx] * NSA_BLOCK + (lane & (NSA_BLOCK - 1))
        dist = (tpos - kpos).astype(F32)
        kt = page[k * hd:(k + 1) * hd, :]
        vt = page[(NSA_KV_HEADS + k) * hd:(NSA_KV_HEADS + k + 1) * hd, :]
        for g in range(NSA_GROUP):
            h = k * NSA_GROUP + g
            qh = qT_ref[:, h:h + 1]
            s = jnp.sum(kt * qh, axis=0, keepdims=True) * scale - slopes[h] * dist
            s = jnp.where(mask, s, NEG_BIG)
            m_old = m_sc[:, h:h + 1]
            m_new = jnp.maximum(m_old, jnp.max(s, axis=-1, keepdims=True))
            alpha = jnp.exp(m_old - m_new)
            p = jnp.where(mask, jnp.exp(s - m_new), 0.0)
            l_sc[:, h:h + 1] = alpha * l_sc[:, h:h + 1] + jnp.sum(p, axis=-1, keepdims=True)
            acc_sc[:, h:h + 1] = alpha * acc_sc[:, h:h + 1] + jnp.sum(vt * p, axis=-1, keepdims=True)
            m_sc[:, h:h + 1] = m_new

    @pl.when(n == n_sel - 1)
    def _fin():
        for h in range(NSA_HEADS):
            k = h // NSA_GROUP
            qh = qT_ref[:, h:h + 1]
            s0 = jnp.sum(newT_ref[:, k:k + 1] * qh, axis=0, keepdims=True) * scale
            m_old = m_sc[:, h:h + 1]
            m_new = jnp.maximum(m_old, s0)
            alpha = jnp.exp(m_old - m_new)
            p0 = jnp.exp(s0 - m_new)
            l = alpha * l_sc[:, h:h + 1] + p0
            acc = alpha * acc_sc[:, h:h + 1] + p0 * newT_ref[:, NSA_KV_HEADS + k:NSA_KV_HEADS + k + 1]
            o_ref[:, h:h + 1] = acc / l


def nsa_sel_sample(qT, newT, cache_t, pid, half, okf, blk, *, tpos, slopes, scale):
    DB = qT.shape[0]
    n_sel = pid.shape[0] // (DB * NSA_KV_HEADS)
    F = cache_t.shape[1]

    def page_map(k):
        return lambda b, n, pid, half, okf, blk: (pid[(b * NSA_KV_HEADS + k) * n_sel + n], 0, 0)

    body = functools.partial(_nsa_sel_sample_body, n_sel=n_sel, tpos=int(tpos),
                             slopes=tuple(float(s) for s in slopes), scale=float(scale))
    return pl.pallas_call(
        body,
        grid_spec=pltpu.PrefetchScalarGridSpec(
            num_scalar_prefetch=4, grid=(DB, n_sel),
            in_specs=[pl.BlockSpec((None, NSA_HD, NSA_HEADS), lambda b, n, *_: (b, 0, 0)),
                      pl.BlockSpec((None, NSA_HD, newT.shape[2]), lambda b, n, *_: (b, 0, 0)),
                      pl.BlockSpec((None, F, PAGE_SIZE), page_map(0)),
                      pl.BlockSpec((None, F, PAGE_SIZE), page_map(1))],
            out_specs=pl.BlockSpec((None, NSA_HD, NSA_HEADS), lambda b, n, *_: (b, 0, 0)),
            scratch_shapes=[pltpu.VMEM((1, NSA_HEADS), F32), pltpu.VMEM((1, NSA_HEADS), F32),
                            pltpu.VMEM((NSA_HD, NSA_HEADS), F32)]),
        out_shape=jax.ShapeDtypeStruct((DB, NSA_HD, NSA_HEADS), F32),
        compiler_params=_params(("parallel", "arbitrary")),
        name="nsa_sel_sample",
    )(pid, half, okf, blk, qT, newT, cache_t, cache_t)


CMP_PAGES = 16


def _nsa_cmp_pages_body(ids_ref, *refs):
    pages = refs[:CMP_PAGES]
    w1_ref, b1_ref, w2_ref, o_ref, stack_sc = refs[CMP_PAGES:]
    F = NSA_KV_HEADS * 2 * NSA_HD
    for i, pg in enumerate(pages):
        stack_sc[i * F:(i + 1) * F, :] = pg[...]
    for c in range(2):
        acc = jnp.zeros((NSA_KV_HEADS * CMP_PAGES, PAGE_SIZE), F32)
        for d in range(NSA_HD):
            lhs = jnp.concatenate(
                [stack_sc[pl.ds((c * NSA_KV_HEADS + k) * NSA_HD + d, CMP_PAGES, stride=F), :]
                 for k in range(NSA_KV_HEADS)], axis=0)
            acc = acc + jnp.dot(lhs.astype(BF16), w1_ref[c, d], preferred_element_type=F32)
        hpre = acc + b1_ref[c]
        hid = 0.5 * hpre * (1.0 + lax.erf(hpre * (2.0 ** -0.5)))
        out = jnp.dot(hid.astype(BF16), w2_ref[c], preferred_element_type=F32)
        for k in range(NSA_KV_HEADS):
            o_ref[c * NSA_KV_HEADS + k] = out[k * CMP_PAGES:(k + 1) * CMP_PAGES, :]


def nsa_compress_pages(cache_t, page_ids, pe, w1, w2):
    NP = page_ids.shape[0]
    assert NP % CMP_PAGES == 0
    F = cache_t.shape[1]
    hd = NSA_HD
    w1_cd = jnp.transpose(w1, (0, 2, 1, 3))
    z = jnp.zeros_like(w1_cd)
    w1_blk = jnp.concatenate([jnp.concatenate([w1_cd, z], -1), jnp.concatenate([z, w1_cd], -1)], -2).astype(BF16)
    bias = jnp.einsum('crd,crde->ce', pe, w1, precision=lax.Precision.HIGHEST)
    b1 = jnp.concatenate([bias, bias], -1)[:, None, :]
    z2 = jnp.zeros_like(w2)
    w2_blk = jnp.concatenate([jnp.concatenate([w2, z2], -1), jnp.concatenate([z2, w2], -1)], -2).astype(BF16)

    def page_map(i):
        return lambda s, ids: (ids[s * CMP_PAGES + i], 0, 0)

    out = pl.pallas_call(
        _nsa_cmp_pages_body,
        grid_spec=pltpu.PrefetchScalarGridSpec(
            num_scalar_prefetch=1, grid=(NP // CMP_PAGES,),
            in_specs=[pl.BlockSpec((None, F, PAGE_SIZE), page_map(i)) for i in range(CMP_PAGES)]
            + [pl.BlockSpec((2, hd, 2 * hd, 2 * hd), lambda s, ids: (0, 0, 0, 0)),
               pl.BlockSpec((2, 1, 2 * hd), lambda s, ids: (0, 0, 0)),
               pl.BlockSpec((2, 2 * hd, 2 * hd), lambda s, ids: (0, 0, 0))],
            out_specs=pl.BlockSpec((2 * NSA_KV_HEADS, CMP_PAGES, 2 * hd), lambda s, ids: (0, s, 0)),
            scratch_shapes=[pltpu.VMEM((CMP_PAGES * F, PAGE_SIZE), F32)]),
        out_shape=jax.ShapeDtypeStruct((2 * NSA_KV_HEADS, NP, 2 * hd), F32),
        compiler_params=_params(("parallel",)),
        name="nsa_compress_pages",
    )(page_ids, *([cache_t] * CMP_PAGES), w1_blk, b1, w2_blk)
    return jnp.transpose(out.reshape(2, NSA_KV_HEADS, NP, 2, hd), (2, 3, 0, 1, 4))


def rms_norm(x, g):
    return x * lax.rsqrt(jnp.mean(x * x, -1, keepdims=True) + RMS_EPS) * g


def alibi_slopes_py(n):
    return [2.0 ** (-8.0 * (i + 1) / n) for i in range(n)]


def rope(x, pos):
    half = x.shape[-1] // 2
    inv = ROPE_THETA ** (-jnp.arange(half, dtype=F32) / half)
    ang = pos.astype(F32)[:, None] * inv
    ang = ang.reshape(ang.shape[:1] + (1,) * (x.ndim - 2) + (half,))
    cos, sin = jnp.cos(ang), jnp.sin(ang)
    x1 = x[..., :half]
    x2 = x[..., half:]
    return jnp.concatenate([x1 * cos - x2 * sin, x2 * cos + x1 * sin], -1)


def masked_softmax(s, mask):
    s = jnp.where(mask, s, -jnp.inf)
    m = jnp.max(s, axis=-1, keepdims=True)
    m = jnp.where(jnp.isfinite(m), m, 0.0)
    e = jnp.exp(s - m)
    den = jnp.maximum(jnp.sum(e, axis=-1, keepdims=True), TINY)
    return e / den, (m + jnp.log(den))[..., 0]


def even_project(x, pos, w_in_b, rms_q, rms_kv, w_uq_b):
    T = x.shape[0]
    h = matmul(x, w_in_b, tn=1024)
    c_q = rms_norm(h[:, :MLA_D_CQ], rms_q)
    c_kv = rms_norm(h[:, MLA_D_CQ:MLA_D_CQ + MLA_D_C], rms_kv)
    k_rope = rope(h[:, MLA_D_CQ + MLA_D_C:MLA_COLS], pos)
    q = matmul(c_q, w_uq_b).reshape(T, MLA_HEADS, MLA_D_QK)
    q_nope = q[..., :MLA_D_NOPE]
    q_rope = rope(q[..., MLA_D_NOPE:], pos)
    o = MLA_COLS
    q_nsa = h[:, o:o + NSA_Q_COLS]
    o += NSA_Q_COLS
    kv_nsa = h[:, o:o + NSA_KV_COLS]
    o += NSA_KV_COLS
    gates = jax.nn.sigmoid(h[:, o:o + NSA_GATE_COLS]).reshape(T, NSA_HEADS, 3)
    return q_nope, q_rope, c_kv, k_rope, q_nsa, kv_nsa, gates


def nsa_compress(kv_rows, pe, w1_b, w2_b):
    nblk = kv_rows.shape[0]
    blocks = kv_rows + jnp.transpose(pe, (1, 0, 2))[None, :, :, None, :]
    out = []
    for c in range(2):
        xc = jnp.transpose(blocks[:, :, c], (0, 2, 1, 3)).reshape(nblk * NSA_KV_HEADS, NSA_BLOCK * NSA_HD)
        hid = matmul(xc, w1_b[c].reshape(NSA_BLOCK * NSA_HD, NSA_HD), tm=256)
        hid = jax.nn.gelu(hid, approximate=False)
        out.append(matmul(hid, w2_b[c]).reshape(nblk, NSA_KV_HEADS, NSA_HD))
    return jnp.stack(out, axis=1)


def even_layer_prompt(x, w_in_b, rms_q, rms_kv, w_uq_b, w_ukv_b, cmp_pe, cmp_w1_b, cmp_w2_b, w_o_b):
    S = x.shape[0]
    pos = jnp.arange(S)
    q_nope, q_rope, c_kv, k_rope, q_nsa, kv_nsa, gates = even_project(x, pos, w_in_b, rms_q, rms_kv, w_uq_b)
    kv_up = matmul(c_kv, w_ukv_b)
    k_nope = kv_up[:, :MLA_HEADS * MLA_D_NOPE].reshape(S, MLA_HEADS, MLA_D_NOPE)
    v_mla = kv_up[:, MLA_HEADS * MLA_D_NOPE:].astype(BF16)
    q_mla = jnp.concatenate([q_nope, q_rope], -1).reshape(S, MLA_HEADS * MLA_D_QK).astype(BF16)
    k_mla = jnp.concatenate([k_nope, jnp.broadcast_to(k_rope[:, None, :], (S, MLA_HEADS, MLA_D_ROPE))], -1)
    k_mla = k_mla.reshape(S, MLA_HEADS * MLA_D_QK).astype(BF16)
    o_mla = flash_attention(q_mla, k_mla, v_mla, H=MLA_HEADS, KVH=MLA_HEADS, Dk=MLA_D_QK, Dv=MLA_D_V,
                            scale=MLA_SCALE, mode="causal")
    kv5 = kv_nsa.reshape(S, 3, 2, NSA_KV_HEADS, NSA_HD)
    n_blocks = S // NSA_BLOCK
    comp = nsa_compress(kv5[:, 0].reshape(n_blocks, NSA_BLOCK, 2, NSA_KV_HEADS, NSA_HD), cmp_pe, cmp_w1_b, cmp_w2_b)
    nbp = _round_up(n_blocks, LANE)
    comp = jnp.pad(comp.reshape(n_blocks, 2, NSA_KV_HEADS * NSA_HD), ((0, nbp - n_blocks), (0, 0), (0, 0))).astype(BF16)
    q_b = q_nsa.astype(BF16)
    kv_b = kv_nsa.astype(BF16)
    o_c, sel = nsa_cmp_select(q_b, comp[:, 0], comp[:, 1], n_blocks)
    slopes = alibi_slopes_py(NSA_HEADS)
    nsa_kw = dict(H=NSA_HEADS, KVH=NSA_KV_HEADS, Dk=NSA_HD, Dv=NSA_HD, scale=NSA_HD ** -0.5, slopes=slopes)
    o_s = flash_attention(q_b, kv_b, kv_b, k_col=lambda r: 2, v_col=lambda r: 3, mode="causal", sel=sel, **nsa_kw)
    o_w = flash_attention(q_b, kv_b, kv_b, k_col=lambda r: 4, v_col=lambda r: 5, mode="band", band=NSA_WINDOW, **nsa_kw)
    g = gates
    o_nsa = (g[..., 0:1] * o_c.reshape(S, NSA_HEADS, NSA_HD) + g[..., 1:2] * o_s.reshape(S, NSA_HEADS, NSA_HD)
             + g[..., 2:3] * o_w.reshape(S, NSA_HEADS, NSA_HD)).reshape(S, NSA_Q_COLS)
    y = matmul(jnp.concatenate([o_mla, o_nsa], -1), w_o_b)
    kv6 = kv5.reshape(S, 3, 2, NSA_KV_HEADS, NSA_HD)
    win_rows = kv6[S - min(NSA_WINDOW, S):, 2]
    return y, (c_kv, k_rope, kv6[:, 0], kv6[:, 1], win_rows)


def mla_sample(q_nope, q_rope, c_new, kr_new, c_past, kr_past, tpos, w_uk, w_uv):
    Q = q_nope.shape[1]
    P = c_past.shape[1]
    q_lat = jnp.einsum('bqhd,chd->bqhc', q_nope, w_uk)
    s_past = jnp.einsum('bqhc,bsc->bhqs', q_lat, c_past) + jnp.einsum('bqhd,bsd->bhqs', q_rope, kr_past)
    s_new = jnp.einsum('bqhc,bsc->bhqs', q_lat, c_new) + jnp.einsum('bqhd,bsd->bhqs', q_rope, kr_new)
    s = jnp.concatenate([s_past, s_new], -1).astype(F32) * MLA_SCALE
    mask = jnp.concatenate([jnp.ones((Q, P), bool), tpos[None, :] <= tpos[:, None]], -1)
    p, _ = masked_softmax(s, mask)
    o_lat = jnp.einsum('bhqs,bsc->bqhc', p[..., :P], c_past) + jnp.einsum('bhqs,bsc->bqhc', p[..., P:], c_new)
    return jnp.einsum('bqhc,chd->bqhd', o_lat, w_uv)


def _feature_major(x, lead):
    nd = x.ndim
    perm = tuple(range(lead)) + tuple(range(lead + 1, nd)) + (lead,)
    xt = jnp.transpose(x, perm)
    return xt.reshape(xt.shape[:lead] + (-1, x.shape[lead]))


def nsa_sample(q, gates, kv_new, cache_cmp, cache_sel, win_buf, page_table, pe, w1, w2, w1_b, w2_b):
    DB = q.shape[0]
    n_pages = page_table.shape[1]
    past_len = n_pages * PAGE_SIZE
    n_bp = past_len // NSA_BLOCK
    bpp = PAGE_SIZE // NSA_BLOCK
    assert win_buf.shape[1] == NSA_WINDOW and past_len >= NSA_WINDOW
    scale = NSA_HD ** -0.5
    slopes = alibi_slopes_py(NSA_HEADS)
    comp_past = nsa_compress_pages(_feature_major(cache_cmp, 1), page_table.reshape(-1), pe, w1, w2)
    comp_past = comp_past.reshape(DB, n_bp, 2, NSA_KV_HEADS, NSA_HD)
    new_blk = jnp.pad(kv_new[:, 0][:, None], ((0, 0), (0, NSA_BLOCK - 1), (0, 0), (0, 0), (0, 0)))
    comp_new = nsa_compress(new_blk, pe, w1_b, w2_b)
    comp = jnp.concatenate([comp_past, comp_new[:, None]], 1)
    n_c = comp.shape[1]
    c_end = (jnp.arange(n_c) + 1) * NSA_BLOCK - 1
    qg = q.reshape(DB, NSA_KV_HEADS, NSA_GROUP, NSA_HD)
    sl = jnp.asarray(slopes, F32).reshape(1, NSA_KV_HEADS, NSA_GROUP, 1)
    dist_c = past_len - c_end
    ok_c = dist_c >= 0
    s_c = jnp.einsum('bkgd,bnkd->bkgn', qg, comp[:, :, 0]) * scale - sl * dist_c.astype(F32)
    p_c, _ = masked_softmax(s_c, ok_c)
    o_c = jnp.einsum('bkgn,bnkd->bkgd', p_c, comp[:, :, 1])
    cur = past_len // NSA_BLOCK
    cand = ok_c & (jnp.arange(n_c) != cur)
    imp = jnp.where(cand, jnp.sum(p_c, axis=2), -jnp.inf)
    top_v, top_i = lax.top_k(imp, min(NSA_N_TOP, n_c))
    past_i = jnp.clip(top_i, 0, n_bp - 1)
    pid = jnp.take_along_axis(jnp.broadcast_to(page_table[:, None, :], (DB, NSA_KV_HEADS, n_pages)), past_i // bpp, axis=2)
    flat = lambda a: a.reshape(-1).astype(jnp.int32)
    qT = jnp.transpose(q.reshape(DB, NSA_HEADS, NSA_HD), (0, 2, 1))
    new_cols = lambda br: jnp.transpose(kv_new[:, br].reshape(DB, 2 * NSA_KV_HEADS, NSA_HD), (0, 2, 1))
    o_s = nsa_sel_sample(qT, new_cols(1), _feature_major(cache_sel, 1), flat(pid), flat(past_i % bpp),
                         flat(jnp.isfinite(top_v) & (top_i < n_bp)), flat(past_i), tpos=past_len, slopes=slopes, scale=scale)
    kv_rows = [(h // NSA_GROUP) * NSA_HD for h in range(NSA_HEADS)]
    o_w, _ = slab_attend(qT, new_cols(2), _feature_major(win_buf, 1), H=NSA_HEADS, hd=NSA_HD, k_off=kv_rows,
                         v_off=[NSA_KV_HEADS * NSA_HD + r for r in kv_rows], nk_col=[h // NSA_GROUP for h in range(NSA_HEADS)],
                         nv_col=[NSA_KV_HEADS + h // NSA_GROUP for h in range(NSA_HEADS)], stride=1, slopes=slopes, scale=scale)
    to_heads = lambda oT: jnp.transpose(oT, (0, 2, 1))
    g = gates
    return g[..., 0:1] * o_c.reshape(DB, NSA_HEADS, NSA_HD) + g[..., 1:2] * to_heads(o_s) + g[..., 2:3] * to_heads(o_w)


def even_layer_sample(x, c_lat, c_kr, c_cmp, c_sel, win_buf, page_table, w_in_b, rms_q, rms_kv, w_uq_b, w_uk, w_uv,
                      cmp_pe, cmp_w1, cmp_w2, cmp_w1_b, cmp_w2_b, w_o_b):
    DB = x.shape[0]
    past_len = page_table.shape[1] * PAGE_SIZE
    pos = jnp.full((DB,), past_len, jnp.int32)
    q_nope, q_rope, c_kv, k_rope, q_nsa, kv_nsa, gates = even_project(x, pos, w_in_b, rms_q, rms_kv, w_uq_b)
    tpos = past_len + jnp.arange(1)
    lat_past = c_lat[page_table].reshape(DB, -1, MLA_D_C)
    kr_past = c_kr[page_table].reshape(DB, -1, MLA_D_ROPE)
    o_mla = mla_sample(q_nope[:, None], q_rope[:, None], c_kv[:, None], k_rope[:, None], lat_past, kr_past, tpos, w_uk, w_uv)
    kv_new = kv_nsa.reshape(DB, 3, 2, NSA_KV_HEADS, NSA_HD)
    o_nsa = nsa_sample(q_nsa, gates, kv_new, c_cmp, c_sel, win_buf, page_table, cmp_pe, cmp_w1, cmp_w2, cmp_w1_b, cmp_w2_b)
    y = matmul(jnp.concatenate([o_mla.reshape(DB, -1), o_nsa.reshape(DB, -1)], -1), w_o_b)
    kv6 = kv_new[:, None]
    return y, (c_kv[:, None], k_rope[:, None], kv6[:, :, 0], kv6[:, :, 1], kv6[:, :, 2])


def odd_layer_prompt(x, w_in_b, w_o_b):
    S = x.shape[0]
    h = matmul(x, w_in_b, tn=1536)
    C = DIL_HEADS * DIL_HD
    slopes = alibi_slopes_py(N_DIL * DIL_HEADS)
    outs, lses, new = [], [], []
    for g, (w, d) in enumerate(DIL_PATTERNS):
        hg = h[:, g * 3 * C:(g + 1) * 3 * C]
        q2 = hg[:, :C].astype(BF16).reshape(S // d, d * C)
        kv2 = hg[:, C:].astype(BF16).reshape(S // d, d * 2 * C)
        o, lse = flash_attention(q2, kv2, kv2, H=DIL_HEADS, KVH=DIL_HEADS, Dk=DIL_HD, Dv=DIL_HD, R=d,
                                 k_col=lambda r: 2 * r, v_col=lambda r: 2 * r + 1, scale=DIL_HD ** -0.5,
                                 slopes=[s * d for s in slopes[g * DIL_HEADS:(g + 1) * DIL_HEADS]],
                                 mode="band", band=w // d, tq=256, tk=256, want_lse=True)
        outs.append(o.reshape(S, DIL_HEADS, DIL_HD))
        lses.append(jnp.transpose(lse, (1, 0, 2)).reshape(S, DIL_HEADS))
        new.append(hg[S - min(w, S):, C:].reshape(min(w, S), 2, DIL_HEADS, DIL_HD))
    wgt = jax.nn.softmax(jnp.stack(lses), axis=0)
    o = jnp.sum(wgt[..., None] * jnp.stack(outs), axis=0).reshape(S, ODD_OUT_COLS)
    return matmul(o, w_o_b), tuple(new)


def odd_layer_sample(x, bufs, past_len, w_in_b, w_o_b):
    DB = x.shape[0]
    h = matmul(x, w_in_b, tn=1536).reshape(DB, N_DIL, 3, DIL_HEADS, DIL_HD)
    slopes = alibi_slopes_py(N_DIL * DIL_HEADS)
    C = DIL_HEADS * DIL_HD
    outs, lses = [], []
    for g, (w, d) in enumerate(DIL_PATTERNS):
        assert bufs[g].shape[1] == w and past_len >= w and w % d == 0 and d & (d - 1) == 0
        qT = jnp.transpose(h[:, g, 0], (0, 2, 1))
        newT = jnp.transpose(h[:, g, 1:].reshape(DB, 2 * DIL_HEADS, DIL_HD), (0, 2, 1))
        oT, lse = slab_attend(qT, newT, _feature_major(bufs[g], 1), H=DIL_HEADS, hd=DIL_HD,
                              k_off=[hh * DIL_HD for hh in range(DIL_HEADS)],
                              v_off=[C + hh * DIL_HD for hh in range(DIL_HEADS)],
                              nk_col=list(range(DIL_HEADS)), nv_col=[DIL_HEADS + hh for hh in range(DIL_HEADS)],
                              stride=d, slopes=slopes[g * DIL_HEADS:(g + 1) * DIL_HEADS], scale=DIL_HD ** -0.5)
        outs.append(jnp.transpose(oT, (0, 2, 1)))
        lses.append(lse[:, 0])
    wgt = jax.nn.softmax(jnp.stack(lses), axis=0)
    o = jnp.sum(wgt[..., None] * jnp.stack(outs), axis=0).reshape(DB, ODD_OUT_COLS)
    y = matmul(o, w_o_b)
    return y, tuple(h[:, g, 1:][:, None] for g in range(N_DIL))


def kernel(x_prompt, x_sample, cache_mla_latent, cache_mla_krope, cache_nsa_cmp, cache_nsa_sel, state_nsa_win, state_dil_w128, state_dil_w512, state_dil_w2048, page_table, ln_g, ln_b, even_w_in, mla_rms_q, mla_rms_kv, mla_w_uq, mla_w_uk, mla_w_uv, nsa_cmp_pe, nsa_cmp_w1, nsa_cmp_w2, even_w_o, odd_w_in, odd_w_o, peer_wq, peer_sub_keys, peer_u, peer_v):
    B, S, D = x_prompt.shape
    DB = x_sample.shape[0]
    assert B == 1 and x_sample.shape[1] == 1
    xp = x_prompt.reshape(S, D)
    xs = x_sample.reshape(DB, D)
    past_len = page_table.shape[1] * PAGE_SIZE
    dil_states = (state_dil_w128, state_dil_w512, state_dil_w2048)
    ev_p = [[] for _ in range(5)]
    ev_s = [[] for _ in range(5)]
    od_p = [[] for _ in range(N_DIL)]
    od_s = [[] for _ in range(N_DIL)]
    for layer in range(DEPTH):
        i = layer // 2
        if layer % 2 == 0:
            in_cols = even_w_in.shape[-1]
            w_in_b = jnp.pad(even_w_in[i], ((0, 0), (0, _round_up(in_cols, 1024) - in_cols))).astype(BF16)
            w_uq_b = mla_w_uq[i].reshape(MLA_D_CQ, MLA_HEADS * MLA_D_QK).astype(BF16)
            w_ukv_b = jnp.concatenate([mla_w_uk[i].reshape(MLA_D_C, -1), mla_w_uv[i].reshape(MLA_D_C, -1)], -1).astype(BF16)
            w1_b = nsa_cmp_w1[i].astype(BF16)
            w2_b = nsa_cmp_w2[i].astype(BF16)
            w_o_b = even_w_o[i].astype(BF16)
            mp, new_p = even_layer_prompt(xp, w_in_b, mla_rms_q[i], mla_rms_kv[i], w_uq_b, w_ukv_b,
                                          nsa_cmp_pe[i], w1_b, w2_b, w_o_b)
            ms, new_s = even_layer_sample(xs, cache_mla_latent[i], cache_mla_krope[i], cache_nsa_cmp[i], cache_nsa_sel[i],
                                          state_nsa_win[i], page_table, w_in_b, mla_rms_q[i], mla_rms_kv[i], w_uq_b,
                                          mla_w_uk[i], mla_w_uv[i], nsa_cmp_pe[i], nsa_cmp_w1[i], nsa_cmp_w2[i],
                                          w1_b, w2_b, w_o_b)
            for j in range(5):
                ev_p[j].append(new_p[j][None])
                ev_s[j].append(new_s[j])
        else:
            w_in_b = odd_w_in[i].reshape(D, -1).astype(BF16)
            w_o_b = odd_w_o[i].astype(BF16)
            mp, new_p = odd_layer_prompt(xp, w_in_b, w_o_b)
            ms, new_s = odd_layer_sample(xs, tuple(st[i] for st in dil_states), past_len, w_in_b, w_o_b)
            for j in range(N_DIL):
                od_p[j].append(new_p[j][None])
                od_s[j].append(new_s[j])
        xp = resid_layer_norm(xp, mp, ln_g[layer, 0], ln_b[layer, 0])
        xs = resid_layer_norm(xs, ms, ln_g[layer, 0], ln_b[layer, 0])
        u_t = peer_u[layer].T.astype(BF16)
        v_b = peer_v[layer].astype(BF16)
        xp = resid_layer_norm(xp, peer(xp, peer_wq[layer], peer_sub_keys[layer], u_t, v_b), ln_g[layer, 1], ln_b[layer, 1])
        xs = resid_layer_norm(xs, peer(xs, peer_wq[layer], peer_sub_keys[layer], u_t, v_b), ln_g[layer, 1], ln_b[layer, 1])
    return (xp.reshape(B, S, D), xs.reshape(DB, 1, D),
            jnp.stack(ev_p[0]), jnp.stack(ev_s[0]),
            jnp.stack(ev_p[1]), jnp.stack(ev_s[1]),
            jnp.stack(ev_p[2]), jnp.stack(ev_s[2]),
            jnp.stack(ev_p[3]), jnp.stack(ev_s[3]),
            jnp.stack(ev_p[4]), jnp.stack(ev_s[4]),
            jnp.stack(od_p[0]), jnp.stack(od_s[0]),
            jnp.stack(od_p[1]), jnp.stack(od_s[1]),
            jnp.stack(od_p[2]), jnp.stack(od_s[2]))
```

```python
import functools
import math

import numpy as np
import jax
import jax.numpy as jnp
from jax import lax
from jax.experimental import pallas as pl
from jax.experimental.pallas import tpu as pltpu

D_MODEL = 1024
DEPTH = 2
PAGE_SIZE = 128

MLA_HEADS = 8
MLA_D_CQ = 384
MLA_D_C = 256
MLA_D_NOPE = 64
MLA_D_ROPE = 32
MLA_D_V = 64
MLA_D_QK = MLA_D_NOPE + MLA_D_ROPE
MLA_SCALE = MLA_D_QK ** -0.5
ROPE_THETA = 10000.0

NSA_HEADS = 8
NSA_KV_HEADS = 2
NSA_GROUP = NSA_HEADS // NSA_KV_HEADS
NSA_HD = 64
NSA_BLOCK = 64
NSA_N_TOP = 15
NSA_WINDOW = 512

DIL_PATTERNS = ((128, 1), (512, 4), (2048, 16))
N_DIL = len(DIL_PATTERNS)
DIL_HEADS = 8
DIL_HD = 64

PEER_HEADS = 8
PEER_N_KEYS = 128
PEER_D_KEY = 128
PEER_TOPK = 16

DN_ALPHA = (2 * DEPTH) ** 0.25
LN_EPS = 1e-5
RMS_EPS = 1e-6
TINY = 1e-30
NEG_BIG = -1e30
LOG2E = math.log2(math.e)

MLA_COLS = MLA_D_CQ + MLA_D_C + MLA_D_ROPE
NSA_Q_COLS = NSA_HEADS * NSA_HD
NSA_KV_COLS = 3 * 2 * NSA_KV_HEADS * NSA_HD
NSA_GATE_COLS = 3 * NSA_HEADS
ODD_OUT_COLS = DIL_HEADS * DIL_HD

LANE = 128
SUBLANE = 8
GATE_PITCH = 128 + SUBLANE
VMEM_LIMIT = 48 * 1024 * 1024

BF16 = jnp.bfloat16
F32 = jnp.float32


def _round_up(x, m):
    return -(-x // m) * m


def _tile(n, pref):
    if n <= pref:
        return n
    t = pref
    while n % t:
        t //= 2
    return t


def _params(sem):
    return pltpu.CompilerParams(dimension_semantics=sem, vmem_limit_bytes=VMEM_LIMIT)


def _resid_ln_body(x_ref, m_ref, g_ref, b_ref, o_ref):
    z = DN_ALPHA * x_ref[...] + m_ref[...]
    mu = jnp.mean(z, -1, keepdims=True)
    zc = z - mu
    var = jnp.mean(zc * zc, -1, keepdims=True)
    o_ref[...] = zc * lax.rsqrt(var + LN_EPS) * g_ref[...] + b_ref[...]


def resid_layer_norm(x, m, g, b):
    T, D = x.shape
    tt = _tile(T, 512)
    return pl.pallas_call(
        _resid_ln_body,
        grid=(T // tt,),
        in_specs=[pl.BlockSpec((tt, D), lambda i: (i, 0)),
                  pl.BlockSpec((tt, D), lambda i: (i, 0)),
                  pl.BlockSpec((1, D), lambda i: (0, 0)),
                  pl.BlockSpec((1, D), lambda i: (0, 0))],
        out_specs=pl.BlockSpec((tt, D), lambda i: (i, 0)),
        out_shape=jax.ShapeDtypeStruct((T, D), F32),
        compiler_params=_params(("parallel",)),
        name="resid_ln",
    )(x, m, g.reshape(1, D), b.reshape(1, D))


def _mm_body(a_ref, b_ref, o_ref):
    o_ref[...] = jnp.dot(a_ref[...].astype(BF16), b_ref[...], preferred_element_type=F32).astype(o_ref.dtype)


def matmul(a, b, *, tm=512, tn=None, out_dtype=F32):
    M, K = a.shape
    N = b.shape[1]
    tm = _tile(M, tm)
    tn = N if tn is None else _tile(N, tn)
    return pl.pallas_call(
        _mm_body,
        grid=(M // tm, N // tn),
        in_specs=[pl.BlockSpec((tm, K), lambda i, j: (i, 0)),
                  pl.BlockSpec((K, tn), lambda i, j: (0, j))],
        out_specs=pl.BlockSpec((tm, tn), lambda i, j: (i, j)),
        out_shape=jax.ShapeDtypeStruct((M, N), out_dtype),
        compiler_params=_params(("parallel", "parallel")),
        name="matmul",
    )(a, b.astype(BF16))


def _flash_body(qi_ref, kj_ref, fl_ref, *refs, H, KVH, Dk, Dv, TQ, TK, slopes, mode, band,
                use_sel, nbp, want_lse):
    refs = list(refs)
    q_ref, k_ref, v_ref = refs[:3]
    refs = refs[3:]
    if use_sel:
        sel_ref, exp_ref = refs[:2]
        refs = refs[2:]
    o_ref = refs[0]
    refs = refs[1:]
    if want_lse:
        lse_ref = refs[0]
        refs = refs[1:]
    m_sc, l_sc, acc_sc = refs
    G = H // KVH
    step = pl.program_id(1)
    qi = qi_ref[step]
    kj = kj_ref[step]
    fl = fl_ref[step]

    @pl.when((fl & 1) != 0)
    def _init():
        m_sc[...] = jnp.full(m_sc.shape, NEG_BIG, F32)
        l_sc[...] = jnp.zeros(l_sc.shape, F32)
        acc_sc[...] = jnp.zeros(acc_sc.shape, F32)

    qpos = qi * TQ + lax.broadcasted_iota(jnp.int32, (TQ, TK), 0)
    kpos = kj * TK + lax.broadcasted_iota(jnp.int32, (TQ, TK), 1)
    rel_i = qpos - kpos
    valid = rel_i >= 0
    if mode == "band":
        valid = valid & (rel_i <= band)
    rel = rel_i.astype(F32)
    for kv in range(KVH):
        k = k_ref[:, kv * Dk:(kv + 1) * Dk]
        v = v_ref[:, kv * Dv:(kv + 1) * Dv]
        ok = valid
        if use_sel:
            hit = jnp.dot(sel_ref[:, kv * nbp:(kv + 1) * nbp], exp_ref[...], preferred_element_type=F32)
            ok = ok & (hit > 0.5)
        bias0 = jnp.where(ok, 0.0, NEG_BIG)
        for g in range(G):
            h = kv * G + g
            q = q_ref[:, h * Dk:(h + 1) * Dk]
            s = lax.dot_general(q, k, (((1,), (1,)), ((), ())), preferred_element_type=F32)
            if slopes[h] != 0.0:
                s = s + (bias0 - (slopes[h] * LOG2E) * rel)
            else:
                s = s + bias0
            m_old = m_sc[h]
            m_new = jnp.maximum(m_old, jnp.max(s, axis=-1, keepdims=True))
            alpha = jnp.exp2(m_old - m_new)
            p = jnp.exp2(s - m_new)
            l_sc[h] = alpha * l_sc[h] + jnp.sum(p, axis=-1, keepdims=True)
            acc_sc[h] = alpha * acc_sc[h] + jnp.dot(p.astype(BF16), v, preferred_element_type=F32)
            m_sc[h] = m_new

    @pl.when((fl & 2) != 0)
    def _fin():
        for h in range(H):
            l = l_sc[h]
            o_ref[:, h * Dv:(h + 1) * Dv] = acc_sc[h] / l
            if want_lse:
                lse_ref[:, h:h + 1] = (m_sc[h] + jnp.log2(l)) * (1.0 / LOG2E)


def flash_attention(q, k, v, *, H, KVH, Dk, Dv, R=1, q_col=None, k_col=None, v_col=None, scale, slopes=None,
                    mode="causal", band=0, sel=None, tq=256, tk=512, want_lse=False):
    q = (q.astype(F32) * (float(scale) * LOG2E)).astype(BF16)
    Sq = q.shape[0]
    Sk = k.shape[0]
    TQ = _tile(Sq, tq)
    TK = _tile(Sk, tk)
    nq = Sq // TQ
    slopes = tuple(0.0 for _ in range(H)) if slopes is None else tuple(float(s) for s in slopes)
    q_col = q_col or (lambda r: r)
    k_col = k_col or (lambda r: r)
    v_col = v_col or (lambda r: r)
    qi_l, kj_l, fl_l = [], [], []
    for i in range(nq):
        hi = ((i + 1) * TQ - 1) // TK
        lo = 0 if mode == "causal" else max(0, (i * TQ - band) // TK)
        for j in range(lo, hi + 1):
            qi_l.append(i)
            kj_l.append(j)
            fl_l.append((1 if j == lo else 0) | (2 if j == hi else 0))
    n_steps = len(qi_l)
    qi_a = jnp.asarray(np.array(qi_l, np.int32))
    kj_a = jnp.asarray(np.array(kj_l, np.int32))
    fl_a = jnp.asarray(np.array(fl_l, np.int32))

    in_specs = [pl.BlockSpec((TQ, H * Dk), lambda r, s, qi, kj, fl: (qi[s], q_col(r))),
                pl.BlockSpec((TK, KVH * Dk), lambda r, s, qi, kj, fl: (kj[s], k_col(r))),
                pl.BlockSpec((TK, KVH * Dv), lambda r, s, qi, kj, fl: (kj[s], v_col(r)))]
    args = [q, k, v]
    use_sel = sel is not None
    nbp = 0
    if use_sel:
        nbp = sel.shape[1] // KVH
        nk = Sk // TK
        blk = (np.arange(nk)[:, None, None] * (TK // NSA_BLOCK) + np.arange(TK)[None, None, :] // NSA_BLOCK)
        expand = (np.arange(nbp)[None, :, None] == blk).astype(np.float32)
        in_specs += [pl.BlockSpec((TQ, KVH * nbp), lambda r, s, qi, kj, fl: (qi[s], 0)),
                     pl.BlockSpec((None, nbp, TK), lambda r, s, qi, kj, fl: (kj[s], 0, 0))]
        args += [sel, jnp.asarray(expand, BF16)]
    out_shape = [jax.ShapeDtypeStruct((Sq, R * H * Dv), F32)]
    out_specs = [pl.BlockSpec((TQ, H * Dv), lambda r, s, qi, kj, fl: (qi[s], r))]
    if want_lse:
        out_shape.append(jax.ShapeDtypeStruct((R, Sq, H), F32))
        out_specs.append(pl.BlockSpec((None, TQ, H), lambda r, s, qi, kj, fl: (r, qi[s], 0)))
    body = functools.partial(_flash_body, H=H, KVH=KVH, Dk=Dk, Dv=Dv, TQ=TQ, TK=TK,
                             slopes=slopes, mode=mode, band=int(band), use_sel=use_sel, nbp=nbp,
                             want_lse=want_lse)
    res = pl.pallas_call(
        body,
        grid_spec=pltpu.PrefetchScalarGridSpec(
            num_scalar_prefetch=3, grid=(R, n_steps), in_specs=in_specs, out_specs=out_specs,
            scratch_shapes=[pltpu.VMEM((H, TQ, 1), F32), pltpu.VMEM((H, TQ, 1), F32),
                            pltpu.VMEM((H, TQ, Dv), F32)]),
        out_shape=out_shape,
        compiler_params=_params(("parallel", "arbitrary")),
        name="flash_" + mode + ("_sel" if use_sel else ""),
    )(qi_a, kj_a, fl_a, *args)
    return res if want_lse else res[0]


def _nsa_cmp_body(q_ref, kc_ref, vc_ref, o_ref, sel_ref, *, TQ, NBP, n_blocks, slopes):
    i = pl.program_id(0)
    tpos = i * TQ + lax.broadcasted_iota(jnp.int32, (TQ, NBP), 0)
    blk = lax.broadcasted_iota(jnp.int32, (TQ, NBP), 1)
    dist_i = tpos - ((blk + 1) * NSA_BLOCK - 1)
    ok = (dist_i >= 0) & (blk < n_blocks)
    dist = dist_i.astype(F32)
    cur = jnp.right_shift(tpos, NSA_BLOCK.bit_length() - 1)
    cand = ok & (blk != cur)
    scale = NSA_HD ** -0.5
    for kv in range(NSA_KV_HEADS):
        kc = kc_ref[:, kv * NSA_HD:(kv + 1) * NSA_HD]
        vc = vc_ref[:, kv * NSA_HD:(kv + 1) * NSA_HD]
        imp = jnp.zeros((TQ, NBP), F32)
        for g in range(NSA_GROUP):
            h = kv * NSA_GROUP + g
            q = q_ref[:, h * NSA_HD:(h + 1) * NSA_HD]
            s = lax.dot_general(q, kc, (((1,), (1,)), ((), ())), preferred_element_type=F32)
            s = s * scale - slopes[h] * dist
            s = jnp.where(ok, s, NEG_BIG)
            m = jnp.max(s, axis=-1, keepdims=True)
            e = jnp.where(ok, jnp.exp(s - m), 0.0)
            den = jnp.maximum(jnp.sum(e, axis=-1, keepdims=True), TINY)
            p = e / den
            o_ref[:, h * NSA_HD:(h + 1) * NSA_HD] = jnp.dot(p.astype(BF16), vc, preferred_element_type=F32)
            imp = imp + p
        work = jnp.where(cand, imp, -jnp.inf)
        picked = jnp.zeros((TQ, NBP), F32)
        for _ in range(NSA_N_TOP):
            mx = jnp.max(work, axis=-1, keepdims=True)
            first = jnp.min(jnp.where(work == mx, blk, NBP), axis=-1, keepdims=True)
            hit = blk == first
            picked = jnp.where(hit & (mx > -jnp.inf), 1.0, picked)
            work = jnp.where(hit, -jnp.inf, work)
        picked = jnp.where(blk == cur, 1.0, picked)
        sel_ref[:, kv * NBP:(kv + 1) * NBP] = picked.astype(BF16)


def nsa_cmp_select(q, kc, vc, n_blocks, *, tq=256):
    S = q.shape[0]
    NBP = kc.shape[0]
    TQ = _tile(S, tq)
    slopes = tuple(2.0 ** (-8.0 * (h + 1) / NSA_HEADS) for h in range(NSA_HEADS))
    body = functools.partial(_nsa_cmp_body, TQ=TQ, NBP=NBP, n_blocks=n_blocks, slopes=slopes)
    return pl.pallas_call(
        body,
        grid=(S // TQ,),
        in_specs=[pl.BlockSpec((TQ, NSA_Q_COLS), lambda i: (i, 0)),
                  pl.BlockSpec((NBP, NSA_KV_HEADS * NSA_HD), lambda i: (0, 0)),
                  pl.BlockSpec((NBP, NSA_KV_HEADS * NSA_HD), lambda i: (0, 0))],
        out_specs=[pl.BlockSpec((TQ, NSA_Q_COLS), lambda i: (i, 0)),
                   pl.BlockSpec((TQ, NSA_KV_HEADS * NBP), lambda i: (i, 0))],
        out_shape=[jax.ShapeDtypeStruct((S, NSA_Q_COLS), F32),
                   jax.ShapeDtypeStruct((S, NSA_KV_HEADS * NBP), BF16)],
        compiler_params=_params(("parallel",)),
        name="nsa_cmp_select",
    )(q, kc, vc)


def _top_rows(work, n_rows, k):
    iota = lax.broadcasted_iota(jnp.int32, work.shape, 0)
    vals, idxs = [], []
    for _ in range(k):
        mx = jnp.max(work, axis=0, keepdims=True)
        first = jnp.min(jnp.where(work == mx, iota, n_rows), axis=0, keepdims=True)
        vals.append(mx)
        idxs.append(first)
        work = jnp.where(iota == first, -jnp.inf, work)
    return jnp.concatenate(vals, axis=0), jnp.concatenate(idxs, axis=0)


def _peer_route_body(qh_ref, sk_ref, a_ref, b_ref, g_ref, *, TT):
    K = PEER_TOPK
    half = PEER_D_KEY // 2
    for h in range(PEER_HEADS):
        tops = []
        for p in range(2):
            c0 = (h * 2 + p) * half
            qhp = qh_ref[:, c0:c0 + half].astype(BF16)
            s = lax.dot_general(sk_ref[p], qhp, (((1,), (1,)), ((), ())), preferred_element_type=F32)
            tops.append(_top_rows(s, PEER_N_KEYS, K))
        (v1, i1), (v2, i2) = tops
        cand = jnp.concatenate([v1[k1:k1 + 1, :] + v2 for k1 in range(K)], axis=0)
        best, ci = _top_rows(cand, K * K, K)
        hi = jnp.right_shift(ci, K.bit_length() - 1)
        lo = jnp.bitwise_and(ci, K - 1)
        a = jnp.zeros((K, TT), jnp.int32)
        b = jnp.zeros((K, TT), jnp.int32)
        for k in range(K):
            a = jnp.where(hi == k, i1[k:k + 1, :], a)
            b = jnp.where(lo == k, i2[k:k + 1, :], b)
        e = jnp.exp(best - jnp.max(best, axis=0, keepdims=True))
        gate = e / jnp.sum(e, axis=0, keepdims=True)
        a_ref[h * K:(h + 1) * K, :] = a.astype(F32)
        b_ref[h * K:(h + 1) * K, :] = b.astype(F32)
        g_ref[h * K:(h + 1) * K, :] = gate


def peer_route(qh, sub_keys):
    T = qh.shape[0]
    TT = _tile(T, 128)
    HK = PEER_HEADS * PEER_TOPK
    out = jax.ShapeDtypeStruct((HK, T), F32)
    return pl.pallas_call(
        functools.partial(_peer_route_body, TT=TT),
        grid=(T // TT,),
        in_specs=[pl.BlockSpec((TT, PEER_HEADS * PEER_D_KEY), lambda i: (i, 0)),
                  pl.BlockSpec((2, PEER_N_KEYS, PEER_D_KEY // 2), lambda i: (0, 0, 0))],
        out_specs=[pl.BlockSpec((HK, TT), lambda i: (0, i))] * 3,
        out_shape=[out, out, out],
        compiler_params=_params(("parallel",)),
        name="peer_route",
    )(qh, sub_keys.astype(BF16))


def _peer_gate_body(a_ref, b_ref, g_ref, o_ref, at_sc, bt_sc, gt_sc, gm_sc, *, TT):
    NK = PEER_N_KEYS
    at_sc[...] = a_ref[...].T
    bt_sc[...] = b_ref[...].T
    gt_sc[...] = g_ref[...].T
    CH = 8

    def chunk(c, carry):
        r0 = pl.multiple_of(c * CH, CH)
        a = at_sc[pl.ds(r0, CH), :]
        b = bt_sc[pl.ds(r0, CH), :]
        g = gt_sc[pl.ds(r0, CH), :]
        key = lax.broadcasted_iota(jnp.int32, (CH, NK, a.shape[1]), 1).astype(F32)
        ga = jnp.where(a[:, None, :] == key, g[:, None, :], 0.0).astype(BF16)
        ob = jnp.where(b[:, None, :] == key, 1.0, 0.0).astype(BF16)
        gm = jnp.einsum('tik,tjk->tij', ga, ob, preferred_element_type=F32)
        for j in range(CH):
            gm_sc[pl.ds(pl.multiple_of((c * CH + j) * GATE_PITCH, SUBLANE), NK), :] = gm[j]
        return carry

    lax.fori_loop(0, TT // CH, chunk, 0)
    for i1 in range(NK):
        o_ref[:, i1 * NK:(i1 + 1) * NK] = gm_sc[pl.ds(i1, TT, stride=GATE_PITCH), :].astype(o_ref.dtype)


def peer_gate_matrix(a, b, g):
    HK, T = a.shape
    TT = _tile(T, 128)
    NE = PEER_N_KEYS * PEER_N_KEYS
    return pl.pallas_call(
        functools.partial(_peer_gate_body, TT=TT),
        grid=(T // TT,),
        in_specs=[pl.BlockSpec((HK, TT), lambda i: (0, i))] * 3,
        out_specs=pl.BlockSpec((TT, NE), lambda i: (i, 0)),
        out_shape=jax.ShapeDtypeStruct((T, NE), BF16),
        scratch_shapes=[pltpu.VMEM((TT, HK), F32)] * 3 + [pltpu.VMEM((TT * GATE_PITCH, PEER_N_KEYS), F32)],
        compiler_params=_params(("parallel",)),
        name="peer_gate_matrix",
    )(a, b, g)


def _peer_dense_body(x_ref, ut_ref, v_ref, g_ref, o_ref, acc_sc):
    j = pl.program_id(1)

    @pl.when(j == 0)
    def _init():
        acc_sc[...] = jnp.zeros(acc_sc.shape, F32)

    h = jnp.dot(x_ref[...], ut_ref[...], preferred_element_type=F32)
    act = 0.5 * h * (1.0 + lax.erf(h * (2.0 ** -0.5)))
    w = (act * g_ref[...].astype(F32)).astype(BF16)
    acc_sc[...] += jnp.dot(w, v_ref[...], preferred_element_type=F32)

    @pl.when(j == pl.num_programs(1) - 1)
    def _fin():
        o_ref[...] = acc_sc[...]


def peer_dense(x, u_t, v, gmat, *, tt=1024, te=1024):
    T, D = x.shape
    NE = v.shape[0]
    TT = _tile(T, tt)
    TE = _tile(NE, te)
    return pl.pallas_call(
        _peer_dense_body,
        grid=(T // TT, NE // TE),
        in_specs=[pl.BlockSpec((TT, D), lambda i, j: (i, 0)),
                  pl.BlockSpec((D, TE), lambda i, j: (0, j)),
                  pl.BlockSpec((TE, D), lambda i, j: (j, 0)),
                  pl.BlockSpec((TT, TE), lambda i, j: (i, j))],
        out_specs=pl.BlockSpec((TT, D), lambda i, j: (i, 0)),
        out_shape=jax.ShapeDtypeStruct((T, D), F32),
        scratch_shapes=[pltpu.VMEM((TT, D), F32)],
        compiler_params=_params(("parallel", "arbitrary")),
        name="peer_dense",
    )(x, u_t, v, gmat)


def peer(x, wq, sub_keys, u_t, v_b):
    T, D = x.shape
    qh = matmul(x, wq.reshape(D, PEER_HEADS * PEER_D_KEY))
    a, b, g = peer_route(qh, sub_keys)
    gmat = peer_gate_matrix(a, b, g)
    return peer_dense(x.astype(BF16), u_t, v_b, gmat)


def _slab_attend_body(qT_ref, newT_ref, st_ref, o_ref, lse_ref, *, H, W, hd, k_off, v_off, nk_col, nv_col,
                      stride, slopes, scale):
    lane = lax.broadcasted_iota(jnp.int32, (1, W), 1)
    dist = (W - lane).astype(F32)
    on_tap = (lane & (stride - 1)) == 0
    for h in range(H):
        qh = qT_ref[:, h:h + 1]
        kt = st_ref[k_off[h]:k_off[h] + hd, :]
        vt = st_ref[v_off[h]:v_off[h] + hd, :]
        s = jnp.sum(kt * qh, axis=0, keepdims=True) * scale - slopes[h] * dist
        s = jnp.where(on_tap, s, NEG_BIG)
        s0 = jnp.sum(newT_ref[:, nk_col[h]:nk_col[h] + 1] * qh, axis=0, keepdims=True) * scale
        m = jnp.maximum(jnp.max(s, axis=-1, keepdims=True), s0)
        p = jnp.where(on_tap, jnp.exp(s - m), 0.0)
        p0 = jnp.exp(s0 - m)
        l = jnp.sum(p, axis=-1, keepdims=True) + p0
        o = jnp.sum(vt * p, axis=-1, keepdims=True) + p0 * newT_ref[:, nv_col[h]:nv_col[h] + 1]
        o_ref[:, h:h + 1] = o / l
        lse_ref[:, h:h + 1] = m + jnp.log(l)


def slab_attend(qT, newT, slab, *, H, hd, k_off, v_off, nk_col, nv_col, stride, slopes, scale):
    DB, F, W = slab.shape
    body = functools.partial(_slab_attend_body, H=H, W=W, hd=hd, k_off=tuple(k_off), v_off=tuple(v_off),
                             nk_col=tuple(nk_col), nv_col=tuple(nv_col), stride=stride,
                             slopes=tuple(float(s) for s in slopes), scale=float(scale))
    return pl.pallas_call(
        body,
        grid=(DB,),
        in_specs=[pl.BlockSpec((None, hd, H), lambda b: (b, 0, 0)),
                  pl.BlockSpec((None, hd, newT.shape[2]), lambda b: (b, 0, 0)),
                  pl.BlockSpec((None, F, W), lambda b: (b, 0, 0))],
        out_specs=[pl.BlockSpec((None, hd, H), lambda b: (b, 0, 0)),
                   pl.BlockSpec((None, 1, H), lambda b: (b, 0, 0))],
        out_shape=[jax.ShapeDtypeStruct((DB, hd, H), F32), jax.ShapeDtypeStruct((DB, 1, H), F32)],
        compiler_params=_params(("parallel",)),
        name="slab_attend",
    )(qT, newT, slab)


def _nsa_sel_sample_body(pid_ref, half_ref, ok_ref, blk_ref, qT_ref, newT_ref, p0_ref, p1_ref, o_ref,
                         m_sc, l_sc, acc_sc, *, n_sel, tpos, slopes, scale):
    b = pl.program_id(0)
    n = pl.program_id(1)
    hd = NSA_HD

    @pl.when(n == 0)
    def _init():
        m_sc[...] = jnp.full(m_sc.shape, NEG_BIG, F32)
        l_sc[...] = jnp.zeros(l_sc.shape, F32)
        acc_sc[...] = jnp.zeros(acc_sc.shape, F32)

    lane = lax.broadcasted_iota(jnp.int32, (1, PAGE_SIZE), 1)
    for k, page in enumerate((p0_ref, p1_ref)):
        idx = (b * NSA_KV_HEADS + k) * n_sel + n
        mask = (jnp.right_shift(lane, NSA_BLOCK.bit_length() - 1) == half_ref[idx]) & (ok_ref[idx] != 0)
        kpos = blk_ref[idx] * NSA_BLOCK + (lane & (NSA_BLOCK - 1))
        dist = (tpos - kpos).astype(F32)
        kt = page[k * hd:(k + 1) * hd, :]
        vt = page[(NSA_KV_HEADS + k) * hd:(NSA_KV_HEADS + k + 1) * hd, :]
        for g in range(NSA_GROUP):
            h = k * NSA_GROUP + g
            qh = qT_ref[:, h:h + 1]
            s = jnp.sum(kt * qh, axis=0, keepdims=True) * scale - slopes[h] * dist
            s = jnp.where(mask, s, NEG_BIG)
            m_old = m_sc[:, h:h + 1]
            m_new = jnp.maximum(m_old, jnp.max(s, axis=-1, keepdims=True))
            alpha = jnp.exp(m_old - m_new)
            p = jnp.where(mask, jnp.exp(s - m_new), 0.0)
            l_sc[:, h:h + 1] = alpha * l_sc[:, h:h + 1] + jnp.sum(p, axis=-1, keepdims=True)
            acc_sc[:, h:h + 1] = alpha * acc_sc[:, h:h + 1] + jnp.sum(vt * p, axis=-1, keepdims=True)
            m_sc[:, h:h + 1] = m_new

    @pl.when(n == n_sel - 1)
    def _fin():
        for h in range(NSA_HEADS):
            k = h // NSA_GROUP
            qh = qT_ref[:, h:h + 1]
            s0 = jnp.sum(newT_ref[:, k:k + 1] * qh, axis=0, keepdims=True) * scale
            m_old = m_sc[:, h:h + 1]
            m_new = jnp.maximum(m_old, s0)
            alpha = jnp.exp(m_old - m_new)
            p0 = jnp.exp(s0 - m_new)
            l = alpha * l_sc[:, h:h + 1] + p0
            acc = alpha * acc_sc[:, h:h + 1] + p0 * newT_ref[:, NSA_KV_HEADS + k:NSA_KV_HEADS + k + 1]
            o_ref[:, h:h + 1] = acc / l


def nsa_sel_sample(qT, newT, cache_t, pid, half, okf, blk, *, tpos, slopes, scale):
    DB = qT.shape[0]
    n_sel = pid.shape[0] // (DB * NSA_KV_HEADS)
    F = cache_t.shape[1]

    def page_map(k):
        return lambda b, n, pid, half, okf, blk: (pid[(b * NSA_KV_HEADS + k) * n_sel + n], 0, 0)

    body = functools.partial(_nsa_sel_sample_body, n_sel=n_sel, tpos=int(tpos),
                             slopes=tuple(float(s) for s in slopes), scale=float(scale))
    return pl.pallas_call(
        body,
        grid_spec=pltpu.PrefetchScalarGridSpec(
            num_scalar_prefetch=4, grid=(DB, n_sel),
            in_specs=[pl.BlockSpec((None, NSA_HD, NSA_HEADS), lambda b, n, *_: (b, 0, 0)),
                      pl.BlockSpec((None, NSA_HD, newT.shape[2]), lambda b, n, *_: (b, 0, 0)),
                      pl.BlockSpec((None, F, PAGE_SIZE), page_map(0)),
                      pl.BlockSpec((None, F, PAGE_SIZE), page_map(1))],
            out_specs=pl.BlockSpec((None, NSA_HD, NSA_HEADS), lambda b, n, *_: (b, 0, 0)),
            scratch_shapes=[pltpu.VMEM((1, NSA_HEADS), F32), pltpu.VMEM((1, NSA_HEADS), F32),
                            pltpu.VMEM((NSA_HD, NSA_HEADS), F32)]),
        out_shape=jax.ShapeDtypeStruct((DB, NSA_HD, NSA_HEADS), F32),
        compiler_params=_params(("parallel", "arbitrary")),
        name="nsa_sel_sample",
    )(pid, half, okf, blk, qT, newT, cache_t, cache_t)


MLA_PAGES = 16


def _mla_sample_body(ids_ref, *refs):
    lat = refs[:MLA_PAGES]
    kr = refs[MLA_PAGES:2 * MLA_PAGES]
    ql_ref, qr_ref, cn_ref, kn_ref, o_ref, m_sc, l_sc, acc_sc = refs[2 * MLA_PAGES:]
    j = pl.program_id(1)

    @pl.when(j == 0)
    def _init():
        m_sc[...] = jnp.full(m_sc.shape, NEG_BIG, F32)
        l_sc[...] = jnp.zeros(l_sc.shape, F32)
        acc_sc[...] = jnp.zeros(acc_sc.shape, F32)

    ql = ql_ref[...]
    qr = qr_ref[...]
    pages = [lat[i][...].astype(BF16) for i in range(MLA_PAGES)]
    s = jnp.concatenate(
        [lax.dot_general(ql, pages[i], (((1,), (1,)), ((), ())), preferred_element_type=F32)
         + jnp.dot(qr, kr[i][...].astype(BF16), preferred_element_type=F32) for i in range(MLA_PAGES)],
        axis=1) * MLA_SCALE
    m_old = m_sc[...]
    m_new = jnp.maximum(m_old, jnp.max(s, axis=-1, keepdims=True))
    alpha = jnp.exp(m_old - m_new)
    p = jnp.exp(s - m_new)
    l_sc[...] = alpha * l_sc[...] + jnp.sum(p, axis=-1, keepdims=True)
    pb = p.astype(BF16)
    pv = jnp.dot(pb[:, :PAGE_SIZE], pages[0], preferred_element_type=F32)
    for i in range(1, MLA_PAGES):
        pv = pv + jnp.dot(pb[:, i * PAGE_SIZE:(i + 1) * PAGE_SIZE], pages[i], preferred_element_type=F32)
    acc_sc[...] = alpha * acc_sc[...] + pv
    m_sc[...] = m_new

    @pl.when(j == pl.num_programs(1) - 1)
    def _fin():
        cn = cn_ref[...].astype(BF16).astype(F32)
        kn = kn_ref[...].astype(BF16).astype(F32)
        s0 = (jnp.sum(ql.astype(F32) * cn, axis=-1, keepdims=True)
              + jnp.sum(qr.astype(F32) * kn, axis=-1, keepdims=True)) * MLA_SCALE
        m_o = m_sc[...]
        m_n = jnp.maximum(m_o, s0)
        a = jnp.exp(m_o - m_n)
        p0 = jnp.exp(s0 - m_n)
        l = a * l_sc[...] + p0
        o_ref[...] = (a * acc_sc[...] + p0.astype(BF16).astype(F32) * cn) / l


def mla_sample_latent(q_lat, q_rope, c_new, kr_new, lat_pool, kr_pool_t, page_ids, n_pages):
    DB, H, DC = q_lat.shape
    DR = q_rope.shape[2]
    assert n_pages % MLA_PAGES == 0

    def lat_map(i):
        return lambda b, j, ids: (ids[b * n_pages + j * MLA_PAGES + i], 0, 0)

    row = lambda b, j, ids: (b, 0, 0)
    return pl.pallas_call(
        _mla_sample_body,
        grid_spec=pltpu.PrefetchScalarGridSpec(
            num_scalar_prefetch=1, grid=(DB, n_pages // MLA_PAGES),
            in_specs=[pl.BlockSpec((None, PAGE_SIZE, DC), lat_map(i)) for i in range(MLA_PAGES)]
            + [pl.BlockSpec((None, DR, PAGE_SIZE), lat_map(i)) for i in range(MLA_PAGES)]
            + [pl.BlockSpec((None, H, DC), row), pl.BlockSpec((None, H, DR), row),
               pl.BlockSpec((None, 1, DC), row), pl.BlockSpec((None, 1, DR), row)],
            out_specs=pl.BlockSpec((None, H, DC), row),
            scratch_shapes=[pltpu.VMEM((H, 1), F32), pltpu.VMEM((H, 1), F32), pltpu.VMEM((H, DC), F32)]),
        out_shape=jax.ShapeDtypeStruct((DB, H, DC), F32),
        compiler_params=_params(("parallel", "arbitrary")),
        name="mla_sample",
    )(page_ids, *([lat_pool] * MLA_PAGES), *([kr_pool_t] * MLA_PAGES),
      q_lat.astype(BF16), q_rope.astype(BF16), c_new[:, None, :], kr_new[:, None, :])


CMP_PAGES = 64


def _nsa_cmp_pages_body(ids_ref, *refs):
    pages = refs[:CMP_PAGES]
    w1_ref, b1_ref, w2_ref, o_ref, stack_sc = refs[CMP_PAGES:]
    F = NSA_KV_HEADS * 2 * NSA_HD
    for i, pg in enumerate(pages):
        stack_sc[i * (F + SUBLANE):i * (F + SUBLANE) + F, :] = pg[...]
    for c in range(2):
        lhs = jnp.concatenate(
            [jnp.concatenate([stack_sc[pl.ds((c * NSA_KV_HEADS + k) * NSA_HD + d, CMP_PAGES, stride=F + SUBLANE), :]
                              for k in range(NSA_KV_HEADS)], axis=0).astype(BF16)
             for d in range(NSA_HD)], axis=1)
        hpre = jnp.dot(lhs, w1_ref[c], preferred_element_type=F32) + b1_ref[c]
        hid = 0.5 * hpre * (1.0 + lax.erf(hpre * (2.0 ** -0.5)))
        out = jnp.dot(hid.astype(BF16), w2_ref[c], preferred_element_type=F32)
        for k in range(NSA_KV_HEADS):
            o_ref[c * NSA_KV_HEADS + k] = out[k * CMP_PAGES:(k + 1) * CMP_PAGES, :]


def nsa_compress_pages(cache_t, page_ids, pe, w1, w2):
    NP = page_ids.shape[0]
    assert NP % CMP_PAGES == 0
    F = cache_t.shape[1]
    hd = NSA_HD
    w1_cd = jnp.transpose(w1, (0, 2, 1, 3))
    z = jnp.zeros_like(w1_cd)
    w1_blk = jnp.concatenate([jnp.concatenate([w1_cd, z], -1), jnp.concatenate([z, w1_cd], -1)], -2).astype(BF16)
    w1_blk = w1_blk.reshape(2, hd * 2 * hd, 2 * hd)
    bias = jnp.einsum('crd,crde->ce', pe, w1, precision=lax.Precision.HIGHEST)
    b1 = jnp.concatenate([bias, bias], -1)[:, None, :]
    z2 = jnp.zeros_like(w2)
    w2_blk = jnp.concatenate([jnp.concatenate([w2, z2], -1), jnp.concatenate([z2, w2], -1)], -2).astype(BF16)

    def page_map(i):
        return lambda s, ids: (ids[s * CMP_PAGES + i], 0, 0)

    out = pl.pallas_call(
        _nsa_cmp_pages_body,
        grid_spec=pltpu.PrefetchScalarGridSpec(
            num_scalar_prefetch=1, grid=(NP // CMP_PAGES,),
            in_specs=[pl.BlockSpec((None, F, PAGE_SIZE), page_map(i)) for i in range(CMP_PAGES)]
            + [pl.BlockSpec((2, hd * 2 * hd, 2 * hd), lambda s, ids: (0, 0, 0)),
               pl.BlockSpec((2, 1, 2 * hd), lambda s, ids: (0, 0, 0)),
               pl.BlockSpec((2, 2 * hd, 2 * hd), lambda s, ids: (0, 0, 0))],
            out_specs=pl.BlockSpec((2 * NSA_KV_HEADS, CMP_PAGES, 2 * hd), lambda s, ids: (0, s, 0)),
            scratch_shapes=[pltpu.VMEM((CMP_PAGES * (F + SUBLANE), PAGE_SIZE), F32)]),
        out_shape=jax.ShapeDtypeStruct((2 * NSA_KV_HEADS, NP, 2 * hd), F32),
        compiler_params=_params(("parallel",)),
        name="nsa_compress_pages",
    )(page_ids, *([cache_t] * CMP_PAGES), w1_blk, b1, w2_blk)
    return jnp.transpose(out.reshape(2, NSA_KV_HEADS, NP, 2, hd), (2, 3, 0, 1, 4))


def rms_norm(x, g):
    return x * lax.rsqrt(jnp.mean(x * x, -1, keepdims=True) + RMS_EPS) * g


def alibi_slopes_py(n):
    return [2.0 ** (-8.0 * (i + 1) / n) for i in range(n)]


def rope(x, pos):
    half = x.shape[-1] // 2
    inv = ROPE_THETA ** (-jnp.arange(half, dtype=F32) / half)
    ang = pos.astype(F32)[:, None] * inv
    ang = ang.reshape(ang.shape[:1] + (1,) * (x.ndim - 2) + (half,))
    cos, sin = jnp.cos(ang), jnp.sin(ang)
    x1 = x[..., :half]
    x2 = x[..., half:]
    return jnp.concatenate([x1 * cos - x2 * sin, x2 * cos + x1 * sin], -1)


def masked_softmax(s, mask):
    s = jnp.where(mask, s, -jnp.inf)
    m = jnp.max(s, axis=-1, keepdims=True)
    m = jnp.where(jnp.isfinite(m), m, 0.0)
    e = jnp.exp(s - m)
    den = jnp.maximum(jnp.sum(e, axis=-1, keepdims=True), TINY)
    return e / den, (m + jnp.log(den))[..., 0]


def even_project(x, pos, w_in_b, rms_q, rms_kv, w_uq_b):
    T = x.shape[0]
    h = matmul(x, w_in_b, tn=1024)
    c_q = rms_norm(h[:, :MLA_D_CQ], rms_q)
    c_kv = rms_norm(h[:, MLA_D_CQ:MLA_D_CQ + MLA_D_C], rms_kv)
    k_rope = rope(h[:, MLA_D_CQ + MLA_D_C:MLA_COLS], pos)
    q = matmul(c_q, w_uq_b).reshape(T, MLA_HEADS, MLA_D_QK)
    q_nope = q[..., :MLA_D_NOPE]
    q_rope = rope(q[..., MLA_D_NOPE:], pos)
    o = MLA_COLS
    q_nsa = h[:, o:o + NSA_Q_COLS]
    o += NSA_Q_COLS
    kv_nsa = h[:, o:o + NSA_KV_COLS]
    o += NSA_KV_COLS
    gates = jax.nn.sigmoid(h[:, o:o + NSA_GATE_COLS]).reshape(T, NSA_HEADS, 3)
    return q_nope, q_rope, c_kv, k_rope, q_nsa, kv_nsa, gates


def nsa_compress(kv_rows, pe, w1_b, w2_b):
    nblk = kv_rows.shape[0]
    blocks = kv_rows + jnp.transpose(pe, (1, 0, 2))[None, :, :, None, :]
    out = []
    for c in range(2):
        xc = jnp.transpose(blocks[:, :, c], (0, 2, 1, 3)).reshape(nblk * NSA_KV_HEADS, NSA_BLOCK * NSA_HD)
        hid = matmul(xc, w1_b[c].reshape(NSA_BLOCK * NSA_HD, NSA_HD), tm=256)
        hid = jax.nn.gelu(hid, approximate=False)
        out.append(matmul(hid, w2_b[c]).reshape(nblk, NSA_KV_HEADS, NSA_HD))
    return jnp.stack(out, axis=1)


def even_layer_prompt(x, w_in_b, rms_q, rms_kv, w_uq_b, w_ukv_b, cmp_pe, cmp_w1_b, cmp_w2_b, w_o_b):
    S = x.shape[0]
    pos = jnp.arange(S)
    q_nope, q_rope, c_kv, k_rope, q_nsa, kv_nsa, gates = even_project(x, pos, w_in_b, rms_q, rms_kv, w_uq_b)
    kv_up = matmul(c_kv, w_ukv_b)
    k_nope = kv_up[:, :MLA_HEADS * MLA_D_NOPE].reshape(S, MLA_HEADS, MLA_D_NOPE)
    v_mla = kv_up[:, MLA_HEADS * MLA_D_NOPE:].astype(BF16)
    q_mla = jnp.concatenate([q_nope, q_rope], -1).reshape(S, MLA_HEADS * MLA_D_QK)
    k_mla = jnp.concatenate([k_nope, jnp.broadcast_to(k_rope[:, None, :], (S, MLA_HEADS, MLA_D_ROPE))], -1)
    k_mla = k_mla.reshape(S, MLA_HEADS * MLA_D_QK).astype(BF16)
    o_mla = flash_attention(q_mla, k_mla, v_mla, H=MLA_HEADS, KVH=MLA_HEADS, Dk=MLA_D_QK, Dv=MLA_D_V,
                            scale=MLA_SCALE, mode="causal", tk=2048)
    kv5 = kv_nsa.reshape(S, 3, 2, NSA_KV_HEADS, NSA_HD)
    n_blocks = S // NSA_BLOCK
    comp = nsa_compress(kv5[:, 0].reshape(n_blocks, NSA_BLOCK, 2, NSA_KV_HEADS, NSA_HD), cmp_pe, cmp_w1_b, cmp_w2_b)
    nbp = _round_up(n_blocks, LANE)
    comp = jnp.pad(comp.reshape(n_blocks, 2, NSA_KV_HEADS * NSA_HD), ((0, nbp - n_blocks), (0, 0), (0, 0))).astype(BF16)
    q_b = q_nsa.astype(BF16)
    kv_b = kv_nsa.astype(BF16)
    o_c, sel = nsa_cmp_select(q_b, comp[:, 0], comp[:, 1], n_blocks)
    slopes = alibi_slopes_py(NSA_HEADS)
    nsa_kw = dict(H=NSA_HEADS, KVH=NSA_KV_HEADS, Dk=NSA_HD, Dv=NSA_HD, scale=NSA_HD ** -0.5, slopes=slopes)
    o_s = flash_attention(q_nsa, kv_b, kv_b, k_col=lambda r: 2, v_col=lambda r: 3, mode="causal", sel=sel, tk=2048, **nsa_kw)
    o_w = flash_attention(q_nsa, kv_b, kv_b, k_col=lambda r: 4, v_col=lambda r: 5, mode="band", band=NSA_WINDOW, **nsa_kw)
    g = gates
    o_nsa = (g[..., 0:1] * o_c.reshape(S, NSA_HEADS, NSA_HD) + g[..., 1:2] * o_s.reshape(S, NSA_HEADS, NSA_HD)
             + g[..., 2:3] * o_w.reshape(S, NSA_HEADS, NSA_HD)).reshape(S, NSA_Q_COLS)
    y = matmul(jnp.concatenate([o_mla, o_nsa], -1), w_o_b)
    kv6 = kv5.reshape(S, 3, 2, NSA_KV_HEADS, NSA_HD)
    win_rows = kv6[S - min(NSA_WINDOW, S):, 2]
    return y, (c_kv, k_rope, kv6[:, 0], kv6[:, 1], win_rows)


def _feature_major(x, lead):
    nd = x.ndim
    perm = tuple(range(lead)) + tuple(range(lead + 1, nd)) + (lead,)
    xt = jnp.transpose(x, perm)
    return xt.reshape(xt.shape[:lead] + (-1, x.shape[lead]))


def nsa_sample(q, gates, kv_new, cache_cmp, cache_sel, win_buf, page_table, pe, w1, w2, w1_b, w2_b):
    DB = q.shape[0]
    n_pages = page_table.shape[1]
    past_len = n_pages * PAGE_SIZE
    n_bp = past_len // NSA_BLOCK
    bpp = PAGE_SIZE // NSA_BLOCK
    assert win_buf.shape[1] == NSA_WINDOW and past_len >= NSA_WINDOW
    scale = NSA_HD ** -0.5
    slopes = alibi_slopes_py(NSA_HEADS)
    comp_past = nsa_compress_pages(_feature_major(cache_cmp, 1), page_table.reshape(-1), pe, w1, w2)
    comp_past = comp_past.reshape(DB, n_bp, 2, NSA_KV_HEADS, NSA_HD)
    new_blk = jnp.pad(kv_new[:, 0][:, None], ((0, 0), (0, NSA_BLOCK - 1), (0, 0), (0, 0), (0, 0)))
    comp_new = nsa_compress(new_blk, pe, w1_b, w2_b)
    comp = jnp.concatenate([comp_past, comp_new[:, None]], 1)
    n_c = comp.shape[1]
    c_end = (jnp.arange(n_c) + 1) * NSA_BLOCK - 1
    qg = q.reshape(DB, NSA_KV_HEADS, NSA_GROUP, NSA_HD)
    sl = jnp.asarray(slopes, F32).reshape(1, NSA_KV_HEADS, NSA_GROUP, 1)
    dist_c = past_len - c_end
    ok_c = dist_c >= 0
    s_c = jnp.einsum('bkgd,bnkd->bkgn', qg, comp[:, :, 0]) * scale - sl * dist_c.astype(F32)
    p_c, _ = masked_softmax(s_c, ok_c)
    o_c = jnp.einsum('bkgn,bnkd->bkgd', p_c, comp[:, :, 1])
    cur = past_len // NSA_BLOCK
    cand = ok_c & (jnp.arange(n_c) != cur)
    imp = jnp.where(cand, jnp.sum(p_c, axis=2), -jnp.inf)
    top_v, top_i = lax.top_k(imp, min(NSA_N_TOP, n_c))
    past_i = jnp.clip(top_i, 0, n_bp - 1)
    pid = jnp.take_along_axis(jnp.broadcast_to(page_table[:, None, :], (DB, NSA_KV_HEADS, n_pages)), past_i // bpp, axis=2)
    flat = lambda a: a.reshape(-1).astype(jnp.int32)
    qT = jnp.transpose(q.reshape(DB, NSA_HEADS, NSA_HD), (0, 2, 1))
    new_cols = lambda br: jnp.transpose(kv_new[:, br].reshape(DB, 2 * NSA_KV_HEADS, NSA_HD), (0, 2, 1))
    o_s = nsa_sel_sample(qT, new_cols(1), _feature_major(cache_sel, 1), flat(pid), flat(past_i % bpp),
                         flat(jnp.isfinite(top_v) & (top_i < n_bp)), flat(past_i), tpos=past_len, slopes=slopes, scale=scale)
    kv_rows = [(h // NSA_GROUP) * NSA_HD for h in range(NSA_HEADS)]
    o_w, _ = slab_attend(qT, new_cols(2), _feature_major(win_buf, 1), H=NSA_HEADS, hd=NSA_HD, k_off=kv_rows,
                         v_off=[NSA_KV_HEADS * NSA_HD + r for r in kv_rows], nk_col=[h // NSA_GROUP for h in range(NSA_HEADS)],
                         nv_col=[NSA_KV_HEADS + h // NSA_GROUP for h in range(NSA_HEADS)], stride=1, slopes=slopes, scale=scale)
    to_heads = lambda oT: jnp.transpose(oT, (0, 2, 1))
    g = gates
    return g[..., 0:1] * o_c.reshape(DB, NSA_HEADS, NSA_HD) + g[..., 1:2] * to_heads(o_s) + g[..., 2:3] * to_heads(o_w)


def even_layer_sample(x, c_lat, c_kr, c_cmp, c_sel, win_buf, page_table, w_in_b, rms_q, rms_kv, w_uq_b, w_uk, w_uv,
                      cmp_pe, cmp_w1, cmp_w2, cmp_w1_b, cmp_w2_b, w_o_b):
    DB = x.shape[0]
    past_len = page_table.shape[1] * PAGE_SIZE
    pos = jnp.full((DB,), past_len, jnp.int32)
    q_nope, q_rope, c_kv, k_rope, q_nsa, kv_nsa, gates = even_project(x, pos, w_in_b, rms_q, rms_kv, w_uq_b)
    q_lat = jnp.einsum('bhd,chd->bhc', q_nope, w_uk)
    o_lat = mla_sample_latent(q_lat, q_rope, c_kv, k_rope, c_lat, _feature_major(c_kr, 1), page_table.reshape(-1),
                              page_table.shape[1])
    o_mla = jnp.einsum('bhc,chd->bhd', o_lat, w_uv)
    kv_new = kv_nsa.reshape(DB, 3, 2, NSA_KV_HEADS, NSA_HD)
    o_nsa = nsa_sample(q_nsa, gates, kv_new, c_cmp, c_sel, win_buf, page_table, cmp_pe, cmp_w1, cmp_w2, cmp_w1_b, cmp_w2_b)
    y = matmul(jnp.concatenate([o_mla.reshape(DB, -1), o_nsa.reshape(DB, -1)], -1), w_o_b)
    kv6 = kv_new[:, None]
    return y, (c_kv[:, None], k_rope[:, None], kv6[:, :, 0], kv6[:, :, 1], kv6[:, :, 2])


def odd_layer_prompt(x, w_in_b, w_o_b):
    S = x.shape[0]
    h = matmul(x, w_in_b, tn=1536)
    C = DIL_HEADS * DIL_HD
    slopes = alibi_slopes_py(N_DIL * DIL_HEADS)
    outs, lses, new = [], [], []
    for g, (w, d) in enumerate(DIL_PATTERNS):
        hg = h[:, g * 3 * C:(g + 1) * 3 * C]
        q2 = hg[:, :C].reshape(S // d, d * C)
        kv2 = hg[:, C:].astype(BF16).reshape(S // d, d * 2 * C)
        o, lse = flash_attention(q2, kv2, kv2, H=DIL_HEADS, KVH=DIL_HEADS, Dk=DIL_HD, Dv=DIL_HD, R=d,
                                 k_col=lambda r: 2 * r, v_col=lambda r: 2 * r + 1, scale=DIL_HD ** -0.5,
                                 slopes=[s * d for s in slopes[g * DIL_HEADS:(g + 1) * DIL_HEADS]],
                                 mode="band", band=w // d, tq=256, tk=256, want_lse=True)
        outs.append(o.reshape(S, DIL_HEADS, DIL_HD))
        lses.append(jnp.transpose(lse, (1, 0, 2)).reshape(S, DIL_HEADS))
        new.append(hg[S - min(w, S):, C:].reshape(min(w, S), 2, DIL_HEADS, DIL_HD))
    wgt = jax.nn.softmax(jnp.stack(lses), axis=0)
    o = jnp.sum(wgt[..., None] * jnp.stack(outs), axis=0).reshape(S, ODD_OUT_COLS)
    return matmul(o, w_o_b), tuple(new)


def odd_layer_sample(x, bufs, past_len, w_in_b, w_o_b):
    DB = x.shape[0]
    h = matmul(x, w_in_b, tn=1536).reshape(DB, N_DIL, 3, DIL_HEADS, DIL_HD)
    slopes = alibi_slopes_py(N_DIL * DIL_HEADS)
    C = DIL_HEADS * DIL_HD
    outs, lses = [], []
    for g, (w, d) in enumerate(DIL_PATTERNS):
        assert bufs[g].shape[1] == w and past_len >= w and w % d == 0 and d & (d - 1) == 0
        qT = jnp.transpose(h[:, g, 0], (0, 2, 1))
        newT = jnp.transpose(h[:, g, 1:].reshape(DB, 2 * DIL_HEADS, DIL_HD), (0, 2, 1))
        oT, lse = slab_attend(qT, newT, _feature_major(bufs[g], 1), H=DIL_HEADS, hd=DIL_HD,
                              k_off=[hh * DIL_HD for hh in range(DIL_HEADS)],
                              v_off=[C + hh * DIL_HD for hh in range(DIL_HEADS)],
                              nk_col=list(range(DIL_HEADS)), nv_col=[DIL_HEADS + hh for hh in range(DIL_HEADS)],
                              stride=d, slopes=slopes[g * DIL_HEADS:(g + 1) * DIL_HEADS], scale=DIL_HD ** -0.5)
        outs.append(jnp.transpose(oT, (0, 2, 1)))
        lses.append(lse[:, 0])
    wgt = jax.nn.softmax(jnp.stack(lses), axis=0)
    o = jnp.sum(wgt[..., None] * jnp.stack(outs), axis=0).reshape(DB, ODD_OUT_COLS)
    y = matmul(o, w_o_b)
    return y, tuple(h[:, g, 1:][:, None] for g in range(N_DIL))


def kernel(x_prompt, x_sample, cache_mla_latent, cache_mla_krope, cache_nsa_cmp, cache_nsa_sel, state_nsa_win, state_dil_w128, state_dil_w512, state_dil_w2048, page_table, ln_g, ln_b, even_w_in, mla_rms_q, mla_rms_kv, mla_w_uq, mla_w_uk, mla_w_uv, nsa_cmp_pe, nsa_cmp_w1, nsa_cmp_w2, even_w_o, odd_w_in, odd_w_o, peer_wq, peer_sub_keys, peer_u, peer_v):
    B, S, D = x_prompt.shape
    DB = x_sample.shape[0]
    assert B == 1 and x_sample.shape[1] == 1
    xp = x_prompt.reshape(S, D)
    xs = x_sample.reshape(DB, D)
    past_len = page_table.shape[1] * PAGE_SIZE
    dil_states = (state_dil_w128, state_dil_w512, state_dil_w2048)
    ev_p = [[] for _ in range(5)]
    ev_s = [[] for _ in range(5)]
    od_p = [[] for _ in range(N_DIL)]
    od_s = [[] for _ in range(N_DIL)]
    for layer in range(DEPTH):
        i = layer // 2
        if layer % 2 == 0:
            in_cols = even_w_in.shape[-1]
            w_in_b = jnp.pad(even_w_in[i], ((0, 0), (0, _round_up(in_cols, 1024) - in_cols))).astype(BF16)
            w_uq_b = mla_w_uq[i].reshape(MLA_D_CQ, MLA_HEADS * MLA_D_QK).astype(BF16)
            w_ukv_b = jnp.concatenate([mla_w_uk[i].reshape(MLA_D_C, -1), mla_w_uv[i].reshape(MLA_D_C, -1)], -1).astype(BF16)
            w1_b = nsa_cmp_w1[i].astype(BF16)
            w2_b = nsa_cmp_w2[i].astype(BF16)
            w_o_b = even_w_o[i].astype(BF16)
            mp, new_p = even_layer_prompt(xp, w_in_b, mla_rms_q[i], mla_rms_kv[i], w_uq_b, w_ukv_b,
                                          nsa_cmp_pe[i], w1_b, w2_b, w_o_b)
            ms, new_s = even_layer_sample(xs, cache_mla_latent[i], cache_mla_krope[i], cache_nsa_cmp[i], cache_nsa_sel[i],
                                          state_nsa_win[i], page_table, w_in_b, mla_rms_q[i], mla_rms_kv[i], w_uq_b,
                                          mla_w_uk[i], mla_w_uv[i], nsa_cmp_pe[i], nsa_cmp_w1[i], nsa_cmp_w2[i],
                                          w1_b, w2_b, w_o_b)
            for j in range(5):
                ev_p[j].append(new_p[j][None])
                ev_s[j].append(new_s[j])
        else:
            w_in_b = odd_w_in[i].reshape(D, -1).astype(BF16)
            w_o_b = odd_w_o[i].astype(BF16)
            mp, new_p = odd_layer_prompt(xp, w_in_b, w_o_b)
            ms, new_s = odd_layer_sample(xs, tuple(st[i] for st in dil_states), past_len, w_in_b, w_o_b)
            for j in range(N_DIL):
                od_p[j].append(new_p[j][None])
                od_s[j].append(new_s[j])
        xp = resid_layer_norm(xp, mp, ln_g[layer, 0], ln_b[layer, 0])
        xs = resid_layer_norm(xs, ms, ln_g[layer, 0], ln_b[layer, 0])
        u_t = peer_u[layer].T.astype(BF16)
        v_b = peer_v[layer].astype(BF16)
        xp = resid_layer_norm(xp, peer(xp, peer_wq[layer], peer_sub_keys[layer], u_t, v_b), ln_g[layer, 1], ln_b[layer, 1])
        xs = resid_layer_norm(xs, peer(xs, peer_wq[layer], peer_sub_keys[layer], u_t, v_b), ln_g[layer, 1], ln_b[layer, 1])
    return (xp.reshape(B, S, D), xs.reshape(DB, 1, D),
            jnp.stack(ev_p[0]), jnp.stack(ev_s[0]),
            jnp.stack(ev_p[1]), jnp.stack(ev_s[1]),
            jnp.stack(ev_p[2]), jnp.stack(ev_s[2]),
            jnp.stack(ev_p[3]), jnp.stack(ev_s[3]),
            jnp.stack(ev_p[4]), jnp.stack(ev_s[4]),
            jnp.stack(od_p[0]), jnp.stack(od_s[0]),
            jnp.stack(od_p[1]), jnp.stack(od_s[1]),
            jnp.stack(od_p[2]), jnp.stack(od_s[2]))
```
